```python
import jax
import jax.numpy as jnp
from jax import lax
import numpy as np


D_MODEL = 2048
BATCH = 4
SEQ = 4096
DEPTH = 2

POOL_WINDOWS = (2, 4, 8, 16)
N_POOL_GROUPS = len(POOL_WINDOWS)
POOL_GROUP_DIM = D_MODEL // N_POOL_GROUPS
HEAD_SIZE = 64
N_HEADS = D_MODEL // HEAD_SIZE
N_DIRS = 2
N_SHIFT_MIX = 6
DECAY_LORA = 96
ICLR_LORA = 96
GATE_LORA = 256
LNX_EPS = 64e-5
D_FF_DENSE = 5632
N_EXPERTS = 8
TOP_K = 2
D_FF_EXPERT = 7168
RMS_EPS = 1e-6

kernel_name = 'hybrid_pool_rwkv7_moe_encoder'


def rmsnorm(x, g):
    xf = x.astype(jnp.float32)
    y = xf * lax.rsqrt(jnp.mean(xf * xf, axis=-1, keepdims=True) + RMS_EPS)
    return (y * g.astype(jnp.float32)).astype(x.dtype)


def window_mean_minus_self(h, window):
    seq = h.shape[1]
    hf = h.astype(jnp.float32)
    csum = jnp.pad(lax.cumsum(hf, axis=1), ((0, 0), (1, 0), (0, 0)))
    pos = jnp.arange(seq)
    lo = jnp.clip(pos - window // 2, 0, seq)
    hi = jnp.clip(pos + window - window // 2, 0, seq)
    total = jnp.take(csum, hi, axis=1) - jnp.take(csum, lo, axis=1)
    count = (hi - lo).astype(jnp.float32)[None, :, None]
    return (total / count - hf).astype(h.dtype)


def pool_mixer(h, pool_w, pool_scale):
    bsz, seq, dim = h.shape
    groups = h.reshape(bsz, seq, N_POOL_GROUPS, POOL_GROUP_DIM)
    pooled = jnp.stack([window_mean_minus_self(groups[:, :, g], POOL_WINDOWS[g])
                        for g in range(N_POOL_GROUPS)], axis=2)
    mixed = jnp.einsum('bsgc,gce->bsge', pooled, pool_w).reshape(bsz, seq, dim)
    return mixed * pool_scale


def wkv_scan(r, decay, k, v, kk, a, inclusive, reverse):
    bsz = r.shape[1]
    state0 = jnp.zeros((bsz, N_HEADS, HEAD_SIZE, HEAD_SIZE), jnp.float32)

    def step(state, inp):
        r_t, w_t, k_t, v_t, kk_t, a_t = inp
        sa = jnp.einsum('bhvk,bhk->bhv', state, -kk_t)
        new = (state * w_t[:, :, None, :]
               + sa[..., None] * (kk_t * a_t)[:, :, None, :]
               + v_t[..., None] * k_t[:, :, None, :])
        read = new if inclusive else state
        y = jnp.einsum('bhvk,bhk->bhv', read, r_t)
        return new, y

    _, ys = lax.scan(step, state0, (r, decay, k, v, kk, a), reverse=reverse)
    return ys


def heads(t):
    return t.reshape(t.shape[:-1] + (N_HEADS, HEAD_SIZE))


def time_major(t):
    return jnp.swapaxes(t.astype(jnp.float32), 0, 1)


def rwkv7_bidir_mixer(h, mix, w_r, w_k, w_v, w_o, decay_w0, decay_w1, decay_w2,
                      iclr_a0, iclr_a1, iclr_a2, gate_g1, gate_g2, k_k, k_a, r_k,
                      lnx_w, lnx_b):
    bsz, seq, dim = h.shape
    f32 = jnp.float32
    prev = jnp.pad(h, ((0, 0), (1, 0), (0, 0)))[:, :-1]
    nxt = jnp.pad(h, ((0, 0), (0, 1), (0, 0)))[:, 1:]
    xx = 0.5 * (prev + nxt) - h
    xr, xw, xk, xv, xa, xg = [h + xx * mix[i] for i in range(N_SHIFT_MIX)]

    r = xr @ w_r
    k = xk @ w_k
    v = xv @ w_v
    w_lora = jnp.einsum('ebsr,erd->ebsd', jnp.tanh(jnp.einsum('bsd,edr->ebsr', xw, decay_w1)), decay_w2)
    w_log = -jax.nn.softplus(-(decay_w0[:, None, None, :].astype(f32) + w_lora.astype(f32))) - 0.5
    decay = jnp.exp(-jnp.exp(w_log))
    a_lora = jnp.einsum('ebsr,erd->ebsd', jnp.einsum('bsd,edr->ebsr', xa, iclr_a1), iclr_a2)
    a = jax.nn.sigmoid(iclr_a0[:, None, None, :].astype(f32) + a_lora.astype(f32))
    g = jax.nn.sigmoid(xg @ gate_g1) @ gate_g2

    kk = heads((k * k_k).astype(f32))
    kk = kk / jnp.maximum(jnp.linalg.norm(kk, axis=-1, keepdims=True), 1e-12)
    k_dir = k[None].astype(f32) * (1.0 + (a - 1.0) * k_a.astype(f32))

    r_h = heads(r.astype(f32))
    v_h = heads(v.astype(f32))
    r_tm, v_tm, kk_tm = time_major(r_h), time_major(v_h), time_major(kk)
    y_fwd = wkv_scan(r_tm, time_major(heads(decay[0])), time_major(heads(k_dir[0])), v_tm, kk_tm,
                     time_major(heads(a[0])), inclusive=True, reverse=False)
    y_bwd = wkv_scan(r_tm, time_major(heads(decay[1])), time_major(heads(k_dir[1])), v_tm, kk_tm,
                     time_major(heads(a[1])), inclusive=False, reverse=True)
    y = jnp.swapaxes(y_fwd + y_bwd, 0, 1)

    mu = jnp.mean(y, axis=-1, keepdims=True)
    var = jnp.mean(jnp.square(y - mu), axis=-1, keepdims=True)
    yn = ((y - mu) * lax.rsqrt(var + LNX_EPS)).reshape(bsz, seq, dim)
    yn = yn * lnx_w.astype(f32) + lnx_b.astype(f32)
    bonus = (jnp.sum(r_h * heads(k_dir[0]) * r_k.astype(f32), axis=-1, keepdims=True) * v_h).reshape(bsz, seq, dim)
    out = ((yn + bonus) * g.astype(f32)).astype(h.dtype)
    return out @ w_o


def swiglu(t, w_gate, w_up, w_down):
    return (jax.nn.silu(t @ w_gate) * (t @ w_up)) @ w_down


def moe_swiglu(h, router, w_gate, w_up, w_down):
    bsz, seq, dim = h.shape
    t = h.reshape(-1, dim)
    logits = (t @ router).astype(jnp.float32)
    top_v, top_i = lax.top_k(logits, TOP_K)
    gates = jax.nn.softmax(top_v, axis=-1)
    combine = jnp.sum(jax.nn.one_hot(top_i, N_EXPERTS, dtype=jnp.float32) * gates[..., None], axis=1)
    combine = combine.astype(t.dtype)
    out = jnp.zeros_like(t)
    for e in range(N_EXPERTS):
        out = out + combine[:, e:e + 1] * swiglu(t, w_gate[e], w_up[e], w_down[e])
    return out.reshape(bsz, seq, dim)


def setup_inputs(seed: int = 0) -> dict:
    key = jax.random.key(seed)
    ks = iter(list(jax.random.split(key, 40)))
    D = D_MODEL

    def nrm(shape, scale):
        return scale * jax.random.normal(next(ks), shape, jnp.float32)

    def unif(shape, lo, hi):
        return jax.random.uniform(next(ks), shape, jnp.float32, lo, hi)

    return {
        'x': nrm((BATCH, SEQ, D), 1.0),
        'l0_norm_mix': 1.0 + nrm((D,), 0.02),
        'l0_pool_w': nrm((N_POOL_GROUPS, POOL_GROUP_DIM, POOL_GROUP_DIM), POOL_GROUP_DIM ** -0.5),
        'l0_pool_scale': 1.0 + nrm((D,), 0.1),
        'l0_norm_ffn': 1.0 + nrm((D,), 0.02),
        'l0_ffn_w_gate': nrm((D, D_FF_DENSE), D ** -0.5),
        'l0_ffn_w_up': nrm((D, D_FF_DENSE), D ** -0.5),
        'l0_ffn_w_down': nrm((D_FF_DENSE, D), D_FF_DENSE ** -0.5),
        'l1_norm_mix': 1.0 + nrm((D,), 0.02),
        'l1_mix': unif((N_SHIFT_MIX, D), 0.0, 1.0),
        'l1_w_r': nrm((D, D), D ** -0.5),
        'l1_w_k': nrm((D, D), D ** -0.5),
        'l1_w_v': nrm((D, D), D ** -0.5),
        'l1_w_o': nrm((D, D), D ** -0.5),
        'l1_decay_w0': unif((N_DIRS, D), -6.0, -1.0),
        'l1_decay_w1': nrm((N_DIRS, D, DECAY_LORA), D ** -0.5),
        'l1_decay_w2': nrm((N_DIRS, DECAY_LORA, D), 0.5 * DECAY_LORA ** -0.5),
        'l1_iclr_a0': nrm((N_DIRS, D), 0.1),
        'l1_iclr_a1': nrm((N_DIRS, D, ICLR_LORA), D ** -0.5),
        'l1_iclr_a2': nrm((N_DIRS, ICLR_LORA, D), 0.5 * ICLR_LORA ** -0.5),
        'l1_gate_g1': nrm((D, GATE_LORA), D ** -0.5),
        'l1_gate_g2': nrm((GATE_LORA, D), GATE_LORA ** -0.5),
        'l1_k_k': 0.85 + nrm((D,), 0.05),
        'l1_k_a': 1.0 + nrm((D,), 0.05),
        'l1_r_k': nrm((N_HEADS, HEAD_SIZE), 0.1),
        'l1_lnx_w': 1.0 + nrm((D,), 0.1),
        'l1_lnx_b': nrm((D,), 0.01),
        'l1_norm_ffn': 1.0 + nrm((D,), 0.02),
        'l1_router': nrm((D, N_EXPERTS), D ** -0.5),
        'l1_moe_w_gate': nrm((N_EXPERTS, D, D_FF_EXPERT), D ** -0.5),
        'l1_moe_w_up': nrm((N_EXPERTS, D, D_FF_EXPERT), D ** -0.5),
        'l1_moe_w_down': nrm((N_EXPERTS, D_FF_EXPERT, D), D_FF_EXPERT ** -0.5),
        'final_norm': 1.0 + nrm((D,), 0.02),
    }


def reference(x, l0_norm_mix, l0_pool_w, l0_pool_scale, l0_norm_ffn, l0_ffn_w_gate, l0_ffn_w_up,
              l0_ffn_w_down, l1_norm_mix, l1_mix, l1_w_r, l1_w_k, l1_w_v, l1_w_o, l1_decay_w0,
              l1_decay_w1, l1_decay_w2, l1_iclr_a0, l1_iclr_a1, l1_iclr_a2, l1_gate_g1, l1_gate_g2,
              l1_k_k, l1_k_a, l1_r_k, l1_lnx_w, l1_lnx_b, l1_norm_ffn, l1_router, l1_moe_w_gate,
              l1_moe_w_up, l1_moe_w_down, final_norm):
    layers = (
        dict(norm_mix=l0_norm_mix, pool_w=l0_pool_w, pool_scale=l0_pool_scale,
             norm_ffn=l0_norm_ffn, w_gate=l0_ffn_w_gate, w_up=l0_ffn_w_up, w_down=l0_ffn_w_down),
        dict(norm_mix=l1_norm_mix, mix=l1_mix, w_r=l1_w_r, w_k=l1_w_k, w_v=l1_w_v, w_o=l1_w_o,
             decay_w0=l1_decay_w0, decay_w1=l1_decay_w1, decay_w2=l1_decay_w2,
             iclr_a0=l1_iclr_a0, iclr_a1=l1_iclr_a1, iclr_a2=l1_iclr_a2,
             gate_g1=l1_gate_g1, gate_g2=l1_gate_g2, k_k=l1_k_k, k_a=l1_k_a, r_k=l1_r_k,
             lnx_w=l1_lnx_w, lnx_b=l1_lnx_b, norm_ffn=l1_norm_ffn, router=l1_router,
             w_gate=l1_moe_w_gate, w_up=l1_moe_w_up, w_down=l1_moe_w_down),
    )
    h = x
    for i in range(DEPTH):
        p = layers[i]
        hn = rmsnorm(h, p['norm_mix'])
        if i % 2 == 0:
            h = h + pool_mixer(hn, p['pool_w'], p['pool_scale'])
        else:
            h = h + rwkv7_bidir_mixer(hn, p['mix'], p['w_r'], p['w_k'], p['w_v'], p['w_o'],
                                      p['decay_w0'], p['decay_w1'], p['decay_w2'],
                                      p['iclr_a0'], p['iclr_a1'], p['iclr_a2'],
                                      p['gate_g1'], p['gate_g2'], p['k_k'], p['k_a'], p['r_k'],
                                      p['lnx_w'], p['lnx_b'])
        hn = rmsnorm(h, p['norm_ffn'])
        if i % 2 == 0:
            h = h + swiglu(hn, p['w_gate'], p['w_up'], p['w_down'])
        else:
            h = h + moe_swiglu(hn, p['router'], p['w_gate'], p['w_up'], p['w_down'])
    return rmsnorm(h, final_norm)
```

```python
import functools

import jax
import jax.numpy as jnp
from jax import lax
from jax.experimental import pallas as pl
from jax.experimental.pallas import tpu as pltpu

F32 = jnp.float32
BF16 = jnp.bfloat16

HEAD_SIZE = 64
LANES = 128
POOL_WINDOWS = (2, 4, 8, 16)
POOL_HALO = 8
TOP_K = 2
RMS_EPS = 1e-6
LNX_EPS = 64e-5
CHUNK = 64
VMEM_LIMIT_BYTES = 56 * 1024 * 1024


def _cparams(sem):
    return pltpu.CompilerParams(dimension_semantics=sem, vmem_limit_bytes=VMEM_LIMIT_BYTES)


def _dot(a, b):
    return jnp.dot(a, b, preferred_element_type=F32)


def _dot_nt(a, b):
    return lax.dot_general(a, b, (((1,), (1,)), ((), ())), preferred_element_type=F32)


def _rms(x, g):
    return x * lax.rsqrt(jnp.mean(x * x, axis=-1, keepdims=True) + RMS_EPS) * g


def _split3(x):
    hi = x.astype(BF16)
    r1 = x - hi.astype(F32)
    mid = r1.astype(BF16)
    lo = (r1 - mid.astype(F32)).astype(BF16)
    return hi, mid, lo


def _head_ones():
    r = lax.broadcasted_iota(jnp.int32, (LANES, LANES), 0) // HEAD_SIZE
    c = lax.broadcasted_iota(jnp.int32, (LANES, LANES), 1) // HEAD_SIZE
    return (r == c).astype(BF16)


def _head_sum(x, ones_bd):
    hi, mid, lo = _split3(x)
    return _dot(hi, ones_bd) + _dot(mid, ones_bd) + _dot(lo, ones_bd)


def _pool_kernel(x_ref, xp_ref, xn_ref, g_ref, pw_ref, ps_ref, o_ref, ext_ref, *, seq, ts, cg):
    i = pl.program_id(1)
    nt = pl.num_programs(1)
    g = g_ref[...]
    x = x_ref[...]
    hn = _rms(x, g)
    has_prev = (i > 0).astype(F32)
    has_next = (i < nt - 1).astype(F32)
    ext_ref[0:POOL_HALO, :] = _rms(xp_ref[...], g) * has_prev
    ext_ref[POOL_HALO:POOL_HALO + ts, :] = hn
    ext_ref[POOL_HALO + ts:POOL_HALO + ts + POOL_HALO, :] = _rms(xn_ref[...], g) * has_next
    pos = i * ts + lax.broadcasted_iota(jnp.int32, (ts, 1), 0)
    for gi, w in enumerate(POOL_WINDOWS):
        cs = slice(gi * cg, (gi + 1) * cg)
        lo_off = -(w // 2)
        hi_off = w - w // 2 - 1
        tot = None
        for o in range(lo_off, hi_off + 1):
            part = ext_ref[POOL_HALO + o:POOL_HALO + o + ts, cs]
            tot = part if tot is None else tot + part
        lo = jnp.maximum(pos + lo_off, 0)
        hi = jnp.minimum(pos + hi_off + 1, seq)
        cnt = (hi - lo).astype(F32)
        pooled = tot / cnt - hn[:, cs]
        mixed = _dot(pooled.astype(BF16), pw_ref[gi])
        o_ref[:, cs] = x[:, cs] + mixed * ps_ref[:, cs]


def _pool_layer(x, g, pool_w, pool_scale):
    bsz, seq, dim = x.shape
    ng = len(POOL_WINDOWS)
    cg = dim // ng
    ts = min(seq, 512)
    nt = seq // ts
    hb = ts // POOL_HALO
    nhb = seq // POOL_HALO
    kern = functools.partial(_pool_kernel, seq=seq, ts=ts, cg=cg)
    return pl.pallas_call(
        kern,
        grid=(bsz, nt),
        in_specs=[
            pl.BlockSpec((None, ts, dim), lambda b, i: (b, i, 0)),
            pl.BlockSpec((None, POOL_HALO, dim), lambda b, i: (b, jnp.maximum(i * hb - 1, 0), 0)),
            pl.BlockSpec((None, POOL_HALO, dim), lambda b, i: (b, jnp.minimum((i + 1) * hb, nhb - 1), 0)),
            pl.BlockSpec((1, dim), lambda b, i: (0, 0)),
            pl.BlockSpec((ng, cg, cg), lambda b, i: (0, 0, 0)),
            pl.BlockSpec((1, dim), lambda b, i: (0, 0)),
        ],
        out_specs=pl.BlockSpec((None, ts, dim), lambda b, i: (b, i, 0)),
        out_shape=jax.ShapeDtypeStruct(x.shape, F32),
        scratch_shapes=[pltpu.VMEM((ts + 2 * POOL_HALO, dim), F32)],
        compiler_params=_cparams(("parallel", "arbitrary")),
        name="pool_layer",
    )(x, x, x, g.reshape(1, dim), pool_w.astype(BF16), pool_scale.reshape(1, dim))


def _ffn_kernel(x_ref, g_ref, wg_ref, wu_ref, wd_ref, o_ref, hn_ref):
    j = pl.program_id(1)

    @pl.when(j == 0)
    def _():
        x = x_ref[...]
        hn_ref[...] = _rms(x, g_ref[...]).astype(BF16)
        o_ref[...] = x

    hn = hn_ref[...]
    gate = _dot(hn, wg_ref[...])
    up = _dot(hn, wu_ref[...])
    act = (gate * jax.nn.sigmoid(gate) * up).astype(BF16)
    o_ref[...] += _dot(act, wd_ref[...])


def _ffn_layer(x2d, g, w_gate, w_up, w_down):
    t, dim = x2d.shape
    ff = w_gate.shape[1]
    tm = min(t, 512)
    tf = 512 if ff % 512 == 0 else ff
    return pl.pallas_call(
        _ffn_kernel,
        grid=(t // tm, ff // tf),
        in_specs=[
            pl.BlockSpec((tm, dim), lambda i, j: (i, 0)),
            pl.BlockSpec((1, dim), lambda i, j: (0, 0)),
            pl.BlockSpec((dim, tf), lambda i, j: (0, j)),
            pl.BlockSpec((dim, tf), lambda i, j: (0, j)),
            pl.BlockSpec((tf, dim), lambda i, j: (j, 0)),
        ],
        out_specs=pl.BlockSpec((tm, dim), lambda i, j: (i, 0)),
        out_shape=jax.ShapeDtypeStruct((t, dim), F32),
        scratch_shapes=[pltpu.VMEM((tm, dim), BF16)],
        compiler_params=_cparams(("parallel", "arbitrary")),
        name="ffn_dense",
    )(x2d, g.reshape(1, dim), w_gate.astype(BF16), w_up.astype(BF16), w_down.astype(BF16))


def _prep_kernel(x_ref, xp_ref, xn_ref, g_ref, mix_ref, *o_refs, ts):
    i = pl.program_id(1)
    nt = pl.num_programs(1)
    g = g_ref[...]
    hn = _rms(x_ref[...], g)
    prev_row = _rms(xp_ref[POOL_HALO - 1:POOL_HALO, :], g) * (i > 0).astype(F32)
    next_row = _rms(xn_ref[0:1, :], g) * (i < nt - 1).astype(F32)
    row = lax.broadcasted_iota(jnp.int32, (ts, 1), 0)
    prev = jnp.where(row == 0, prev_row, pltpu.roll(hn, 1, axis=0))
    nxt = jnp.where(row == ts - 1, next_row, pltpu.roll(hn, ts - 1, axis=0))
    xx = 0.5 * (prev + nxt) - hn
    for n, o_ref in enumerate(o_refs):
        o_ref[...] = (hn + xx * mix_ref[n:n + 1, :]).astype(BF16)


def _rwkv_prep(h, g, mix):
    bsz, seq, dim = h.shape
    nmix = mix.shape[0]
    ts = min(seq, 512)
    nt = seq // ts
    hb = ts // POOL_HALO
    nhb = seq // POOL_HALO
    spec = pl.BlockSpec((None, ts, dim), lambda b, i: (b, i, 0))
    return pl.pallas_call(
        functools.partial(_prep_kernel, ts=ts),
        grid=(bsz, nt),
        in_specs=[
            spec,
            pl.BlockSpec((None, POOL_HALO, dim), lambda b, i: (b, jnp.maximum(i * hb - 1, 0), 0)),
            pl.BlockSpec((None, POOL_HALO, dim), lambda b, i: (b, jnp.minimum((i + 1) * hb, nhb - 1), 0)),
            pl.BlockSpec((1, dim), lambda b, i: (0, 0)),
            pl.BlockSpec((nmix, dim), lambda b, i: (0, 0)),
        ],
        out_specs=[spec] * nmix,
        out_shape=[jax.ShapeDtypeStruct(h.shape, BF16)] * nmix,
        compiler_params=_cparams(("parallel", "arbitrary")),
        name="rwkv_prep",
    )(h, h, h, g.reshape(1, dim), mix)


def _mm_kernel(x_ref, w_ref, o_ref):
    o_ref[...] = _dot(x_ref[...], w_ref[...])


def _mm_key_kernel(x_ref, w_ref, kk_scale_ref, k_ref, kk_ref):
    k = _dot(x_ref[...], w_ref[...])
    k_ref[...] = k
    ones_bd = _head_ones()
    for c in range(k.shape[1] // LANES):
        cs = slice(c * LANES, (c + 1) * LANES)
        kq = k[:, cs] * kk_scale_ref[:, cs]
        ss = _head_sum(kq * kq, ones_bd)
        kk_ref[:, cs] = kq * lax.rsqrt(jnp.maximum(ss, 1e-24))


def _proj(x2d, w, kk_scale=None):
    t, kdim = x2d.shape
    n = w.shape[1]
    tm = min(t, 1024)
    tn = min(n, 1024)
    x_spec = pl.BlockSpec((tm, kdim), lambda i, j: (i, 0))
    w_spec = pl.BlockSpec((kdim, tn), lambda i, j: (0, j))
    o_spec = pl.BlockSpec((tm, tn), lambda i, j: (i, j))
    o_shape = jax.ShapeDtypeStruct((t, n), F32)
    if kk_scale is None:
        return pl.pallas_call(
            _mm_kernel, grid=(t // tm, n // tn), in_specs=[x_spec, w_spec], out_specs=o_spec,
            out_shape=o_shape, compiler_params=_cparams(("parallel", "arbitrary")), name="rwkv_proj",
        )(x2d, w.astype(BF16))
    return pl.pallas_call(
        _mm_key_kernel, grid=(t // tm, n // tn),
        in_specs=[x_spec, w_spec, pl.BlockSpec((1, tn), lambda i, j: (0, j))],
        out_specs=[o_spec, o_spec], out_shape=[o_shape, o_shape],
        compiler_params=_cparams(("parallel", "arbitrary")), name="rwkv_proj_key",
    )(x2d, w.astype(BF16), kk_scale.reshape(1, n))


def _lora_kernel(x_ref, w1_ref, w2_ref, b_ref, o_ref, *, mode, hid):
    h = _dot(x_ref[...], w1_ref[...])
    if mode == "decay":
        h = jnp.tanh(h)
    elif mode == "gate":
        h = jax.nn.sigmoid(h)
    h = h.astype(BF16)
    for e in range(o_ref.shape[0]):
        z = _dot(h[:, e * hid:(e + 1) * hid], w2_ref[e])
        if mode == "decay":
            z = -jax.nn.sigmoid(b_ref[e] + z) * jnp.exp(F32(-0.5))
        elif mode == "iclr":
            z = jax.nn.sigmoid(b_ref[e] + z)
        o_ref[e] = z


def _lora(x2d, w1, w2, bias, mode):
    t, dim = x2d.shape
    ne, _, rank = w1.shape
    hid = -(-rank // LANES) * LANES
    w1p = jnp.pad(w1, ((0, 0), (0, 0), (0, hid - rank)))
    w1p = jnp.transpose(w1p, (1, 0, 2)).reshape(dim, ne * hid).astype(BF16)
    w2p = jnp.pad(w2, ((0, 0), (0, hid - rank), (0, 0))).astype(BF16)
    tm = min(t, 512)
    return pl.pallas_call(
        functools.partial(_lora_kernel, mode=mode, hid=hid),
        grid=(t // tm,),
        in_specs=[
            pl.BlockSpec((tm, dim), lambda i: (i, 0)),
            pl.BlockSpec((dim, ne * hid), lambda i: (0, 0)),
            pl.BlockSpec((ne, hid, dim), lambda i: (0, 0, 0)),
            pl.BlockSpec((ne, 1, dim), lambda i: (0, 0, 0)),
        ],
        out_specs=pl.BlockSpec((ne, tm, dim), lambda i: (0, i, 0)),
        out_shape=jax.ShapeDtypeStruct((ne, t, dim), F32),
        compiler_params=_cparams(("parallel",)),
        name="rwkv_lora_" + mode,
    )(x2d, w1p, w2p, bias.reshape(ne, 1, dim))


def _pair_blockdiag(x):
    lane = lax.broadcasted_iota(jnp.int32, x.shape, 1)
    top = jnp.where(lane < HEAD_SIZE, x, 0.0)
    bot = jnp.where(lane >= HEAD_SIZE, x, 0.0)
    return jnp.concatenate([top, bot], axis=0).astype(BF16)


def _pair_dot(a, b):
    return _dot(a.astype(BF16), _pair_blockdiag(b))


def _scan_intra_kernel(r_ref, k_ref, v_ref, kk_ref, lw_ref, a_ref, ka_ref,
                       at_ref, rt_ref, bb_ref, kb_ref, ui_ref, yi_ref, wt_ref):
    d = pl.program_id(0)
    sgn = 1 - 2 * d
    fwd_f = (1 - d).astype(F32)
    L = CHUNK
    t_sq = lax.broadcasted_iota(jnp.int32, (L, L), 0)
    s_sq = lax.broadcasted_iota(jnp.int32, (L, L), 1)
    tri = (((t_sq - s_sq) * sgn) >= 0).astype(BF16)
    t_p = lax.broadcasted_iota(jnp.int32, (L, LANES), 0)
    s_p = lax.broadcasted_iota(jnp.int32, (L, LANES), 1) % HEAD_SIZE
    rel = (t_p - s_p) * sgn
    m_strict = rel > 0
    m_read = (rel + (1 - d)) > 0
    eye = (t_p == s_p).astype(F32)
    for p in range(r_ref.shape[1] // LANES):
        cs = slice(p * LANES, (p + 1) * LANES)
        lw = lw_ref[:, cs]
        hi, mid, lo = _split3(lw)
        c_incl = _dot(tri, hi) + _dot(tri, mid) + _dot(tri, lo)
        c_excl = c_incl - lw
        tot = jnp.sum(lw, axis=0, keepdims=True)
        e_excl = jnp.exp(c_excl)
        e_read = jnp.exp(c_excl + fwd_f * lw)
        e_neg = jnp.exp(-c_incl)
        e_bar = jnp.exp(tot - c_incl)
        kk = kk_ref[:, cs]
        a = a_ref[:, cs]
        v = v_ref[:, cs]
        beta = kk * a
        kt = k_ref[:, cs] * (1.0 + (a - 1.0) * ka_ref[:, cs])
        ah = -kk * e_excl
        rh = r_ref[:, cs] * e_read
        lhs = jnp.concatenate([ah, rh], axis=0).astype(BF16)
        a_b = _dot_nt(lhs, _pair_blockdiag(beta * e_neg))
        a_k = _dot_nt(lhs, _pair_blockdiag(kt * e_neg))
        n_ab = jnp.where(m_strict, a_b[:L], 0.0)
        n_ak = jnp.where(m_strict, a_k[:L], 0.0)
        n_rb = jnp.where(m_read, a_b[L:], 0.0)
        n_rk = jnp.where(m_read, a_k[L:], 0.0)
        tm = eye + n_ab
        pw = n_ab
        span = 1
        while span * 2 < L:
            pw = _pair_dot(pw, pw)
            tm = tm + _pair_dot(pw, tm)
            span *= 2
        at = _pair_dot(tm, ah)
        ui = _pair_dot(tm, _pair_dot(n_ak, v))
        rt = rh + _pair_dot(n_rb, at)
        yi = _pair_dot(n_rb, ui) + _pair_dot(n_rk, v)
        at_ref[:, cs] = at.astype(BF16)
        rt_ref[:, cs] = rt.astype(BF16)
        bb_ref[:, cs] = (beta * e_bar).astype(BF16)
        kb_ref[:, cs] = (kt * e_bar).astype(BF16)
        ui_ref[:, cs] = ui
        yi_ref[:, cs] = yi
        wt_ref[:, cs] = jnp.exp(tot)


def _scan_intra(r, k, v, kk, lw, a, k_a):
    bsz, seq, dim = r.shape
    nc = seq // CHUNK
    cw = min(dim, 1024)
    nq = dim // cw
    grid = (2, bsz, nc, nq)
    tok = pl.BlockSpec((None, CHUNK, cw), lambda d, b, c, q: (b, c, q))
    dirtok = pl.BlockSpec((None, None, CHUNK, cw), lambda d, b, c, q: (d, b, c, q))
    wt_spec = pl.BlockSpec((None, None, None, 1, cw), lambda d, b, c, q: (d, b, c, 0, q))
    big = lambda dt: jax.ShapeDtypeStruct((2, bsz, seq, dim), dt)
    return pl.pallas_call(
        _scan_intra_kernel,
        grid=grid,
        in_specs=[tok, tok, tok, tok, dirtok, dirtok, pl.BlockSpec((1, cw), lambda d, b, c, q: (0, q))],
        out_specs=[dirtok, dirtok, dirtok, dirtok, dirtok, dirtok, wt_spec],
        out_shape=[big(BF16), big(BF16), big(BF16), big(BF16), big(F32), big(F32),
                   jax.ShapeDtypeStruct((2, bsz, nc, 1, dim), F32)],
        compiler_params=_cparams(("parallel", "parallel", "parallel", "arbitrary")),
        name="wkv_intra",
    )(r, k, v, kk, lw, a, k_a.reshape(1, dim))


def _scan_seq_kernel(at_ref, rt_ref, bb_ref, kb_ref, ui_ref, yi_ref, v_ref, wt_ref, y_ref, s_ref):
    c = pl.program_id(2)
    L = CHUNK

    @pl.when(c == 0)
    def _():
        s_ref[...] = jnp.zeros_like(s_ref)

    row_h = lax.broadcasted_iota(jnp.int32, (LANES, LANES), 0) // HEAD_SIZE
    col_h = lax.broadcasted_iota(jnp.int32, (LANES, LANES), 1) // HEAD_SIZE
    same_head = row_h == col_h
    for p in range(s_ref.shape[0]):
        cs = slice(p * LANES, (p + 1) * LANES)
        s = s_ref[p]
        lhs = jnp.concatenate([at_ref[:, cs], rt_ref[:, cs]], axis=0)
        uy = _dot_nt(lhs, s.astype(BF16))
        u = uy[:L] + ui_ref[:, cs]
        y_ref[:, cs] = uy[L:] + yi_ref[:, cs]
        uv_t = jnp.concatenate([u, v_ref[:, cs]], axis=0).T.astype(BF16)
        rhs = jnp.concatenate([bb_ref[:, cs], kb_ref[:, cs]], axis=0)
        upd = _dot(uv_t, rhs)
        s_ref[p] = s * wt_ref[:, cs] + jnp.where(same_head, upd, 0.0)


def _scan_seq(at, rt, bb, kb, ui, yi, v, wt):
    _, bsz, seq, dim = at.shape
    nc = seq // CHUNK

    def cidx(d, c):
        return c + d * (nc - 1 - 2 * c)

    dirtok = pl.BlockSpec((None, None, CHUNK, dim), lambda d, b, c: (d, b, cidx(d, c), 0))
    return pl.pallas_call(
        _scan_seq_kernel,
        grid=(2, bsz, nc),
        in_specs=[dirtok, dirtok, dirtok, dirtok, dirtok, dirtok,
                  pl.BlockSpec((None, CHUNK, dim), lambda d, b, c: (b, cidx(d, c), 0)),
                  pl.BlockSpec((None, None, None, 1, dim), lambda d, b, c: (d, b, cidx(d, c), 0, 0))],
        out_specs=dirtok,
        out_shape=jax.ShapeDtypeStruct((2, bsz, seq, dim), F32),
        scratch_shapes=[pltpu.VMEM((dim // LANES, LANES, LANES), F32)],
        compiler_params=_cparams(("parallel", "parallel", "arbitrary")),
        name="wkv_seq",
    )(at, rt, bb, kb, ui, yi, v, wt)


def _rwkv_out_kernel(yf_ref, yb_ref, r_ref, k_ref, v_ref, a_ref, g_ref, h_ref,
                     ka_ref, rk_ref, lw_ref, lb_ref, wo_ref, o_ref, z_ref):
    ones_bd = _head_ones()
    inv_n = 1.0 / HEAD_SIZE
    for c in range(h_ref.shape[1] // LANES):
        cs = slice(c * LANES, (c + 1) * LANES)
        y = yf_ref[:, cs] + yb_ref[:, cs]
        mu = _head_sum(y, ones_bd) * inv_n
        yc = y - mu
        var = _head_sum(yc * yc, ones_bd) * inv_n
        yn = yc * lax.rsqrt(var + LNX_EPS) * lw_ref[:, cs] + lb_ref[:, cs]
        kd = k_ref[:, cs] * (1.0 + (a_ref[:, cs] - 1.0) * ka_ref[:, cs])
        bonus = _head_sum(r_ref[:, cs] * kd * rk_ref[:, cs], ones_bd) * v_ref[:, cs]
        z_ref[:, cs] = ((yn + bonus) * g_ref[:, cs]).astype(BF16)
    o_ref[...] = h_ref[...] + _dot(z_ref[...], wo_ref[...])


def _rwkv_out(y, r, k, v, a, g, h2d, k_a, r_k, lnx_w, lnx_b, w_o):
    t, dim = h2d.shape
    tm = min(t, 256)
    tok = pl.BlockSpec((tm, dim), lambda i: (i, 0))
    vec = pl.BlockSpec((1, dim), lambda i: (0, 0))
    return pl.pallas_call(
        _rwkv_out_kernel,
        grid=(t // tm,),
        in_specs=[pl.BlockSpec((None, tm, dim), lambda i: (0, i, 0)),
                  pl.BlockSpec((None, tm, dim), lambda i: (1, i, 0)),
                  tok, tok, tok,
                  pl.BlockSpec((None, tm, dim), lambda i: (0, i, 0)),
                  tok, tok, vec, vec, vec, vec,
                  pl.BlockSpec((dim, dim), lambda i: (0, 0))],
        out_specs=tok,
        out_shape=jax.ShapeDtypeStruct((t, dim), F32),
        scratch_shapes=[pltpu.VMEM((tm, dim), BF16)],
        compiler_params=_cparams(("parallel",)),
        name="rwkv_out",
    )(y, y, r, k, v, a, g, h2d, k_a.reshape(1, dim), r_k.reshape(1, dim),
      lnx_w.reshape(1, dim), lnx_b.reshape(1, dim), w_o.astype(BF16))


ROUTE_E1, ROUTE_E2, ROUTE_G1, ROUTE_G2, ROUTE_R1, ROUTE_R2 = range(6)


def _router_kernel(x_ref, g_ref, w_ref, hn_ref, route_ref, cnt_ref, *, n_exp, tm):
    i = pl.program_id(0)

    @pl.when(i == 0)
    def _():
        cnt_ref[...] = jnp.zeros_like(cnt_ref)

    hn = _rms(x_ref[...], g_ref[...])
    hn_ref[...] = hn.astype(BF16)
    x_hi, x_mid, x_lo = _split3(hn)
    w_hi, w_mid, w_lo = w_ref[0], w_ref[1], w_ref[2]
    logits = (_dot(x_hi, w_hi) + _dot(x_hi, w_mid) + _dot(x_mid, w_hi)
              + _dot(x_hi, w_lo) + _dot(x_mid, w_mid) + _dot(x_lo, w_hi))
    lane = lax.broadcasted_iota(jnp.int32, (tm, LANES), 1)
    neg = F32(-jnp.inf)
    lg = jnp.where(lane < n_exp, logits, neg)
    m1 = jnp.max(lg, axis=-1, keepdims=True)
    i1 = jnp.min(jnp.where(lg == m1, lane, LANES), axis=-1, keepdims=True)
    lg2 = jnp.where(lane == i1, neg, lg)
    m2 = jnp.max(lg2, axis=-1, keepdims=True)
    i2 = jnp.min(jnp.where(lg2 == m2, lane, LANES), axis=-1, keepdims=True)
    e2 = jnp.exp(m2 - m1)
    den = 1.0 + e2
    g1 = 1.0 / den
    g2 = e2 / den
    oh1 = lane == i1
    oh2 = lane == i2
    oh = oh1.astype(F32) + oh2.astype(F32)
    rr = lax.broadcasted_iota(jnp.int32, (tm, tm), 0)
    cc = lax.broadcasted_iota(jnp.int32, (tm, tm), 1)
    before = (cc < rr).astype(BF16)
    prefix = _dot(before, oh.astype(BF16)) + cnt_ref[...]
    r1 = jnp.sum(jnp.where(oh1, prefix, 0.0), axis=-1, keepdims=True)
    r2 = jnp.sum(jnp.where(oh2, prefix, 0.0), axis=-1, keepdims=True)
    cnt_ref[...] += jnp.sum(oh, axis=0, keepdims=True)
    route = jnp.zeros((tm, LANES), F32)
    for col, val in ((ROUTE_E1, i1.astype(F32)), (ROUTE_E2, i2.astype(F32)), (ROUTE_G1, g1),
                     (ROUTE_G2, g2), (ROUTE_R1, r1), (ROUTE_R2, r2)):
        route = jnp.where(lane == col, val, route)
    route_ref[...] = route


def _router(h2d, g, router_w):
    t, dim = h2d.shape
    n_exp = router_w.shape[1]
    tm = min(t, 512)
    wp = jnp.pad(router_w, ((0, 0), (0, LANES - n_exp)))
    w3 = jnp.stack(_split3(wp))
    return pl.pallas_call(
        functools.partial(_router_kernel, n_exp=n_exp, tm=tm),
        grid=(t // tm,),
        in_specs=[pl.BlockSpec((tm, dim), lambda i: (i, 0)),
                  pl.BlockSpec((1, dim), lambda i: (0, 0)),
                  pl.BlockSpec((3, dim, LANES), lambda i: (0, 0, 0))],
        out_specs=[pl.BlockSpec((tm, dim), lambda i: (i, 0)),
                   pl.BlockSpec((tm, LANES), lambda i: (i, 0)),
                   pl.BlockSpec((1, LANES), lambda i: (0, 0))],
        out_shape=[jax.ShapeDtypeStruct((t, dim), BF16),
                   jax.ShapeDtypeStruct((t, LANES), F32),
                   jax.ShapeDtypeStruct((1, LANES), F32)],
        compiler_params=_cparams(("arbitrary",)),
        name="moe_router",
    )(h2d, g.reshape(1, dim), w3)


def _dispatch_kernel(slots_ref, hn_ref, init_ref, xs_ref, sem, *, rows):
    del init_ref
    base = pl.program_id(0) * rows

    def copy(t, k):
        return pltpu.make_async_copy(hn_ref.at[pl.ds(base + t, 1)],
                                     xs_ref.at[pl.ds(slots_ref[TOP_K * (base + t) + k], 1)], sem)

    def issue(t, carry):
        for k in range(TOP_K):
            copy(t, k).start()
        return carry

    def drain(t, carry):
        for k in range(TOP_K):
            copy(t, k).wait()
        return carry

    lax.fori_loop(0, rows, issue, 0)
    lax.fori_loop(0, rows, drain, 0)


def _dispatch(hn, slots, n_rows):
    t = hn.shape[0]
    tile_shape = hn.shape[1:]
    rows = min(t, 256)
    return pl.pallas_call(
        functools.partial(_dispatch_kernel, rows=rows),
        grid_spec=pltpu.PrefetchScalarGridSpec(
            num_scalar_prefetch=1,
            grid=(t // rows,),
            in_specs=[pl.BlockSpec(memory_space=pl.ANY), pl.BlockSpec(memory_space=pl.ANY)],
            out_specs=pl.BlockSpec(memory_space=pl.ANY),
            scratch_shapes=[pltpu.SemaphoreType.DMA(())],
        ),
        out_shape=jax.ShapeDtypeStruct((n_rows,) + tile_shape, BF16),
        input_output_aliases={2: 0},
        compiler_params=_cparams(("arbitrary",)),
        name="moe_dispatch",
    )(slots, hn, jnp.zeros((n_rows,) + tile_shape, BF16))


def _expert_kernel(te_ref, nu_ref, x_ref, wg_ref, wu_ref, wd_ref, o_ref):
    del te_ref
    i = pl.program_id(0)
    j = pl.program_id(1)
    used = i < nu_ref[0]

    @pl.when(j == 0)
    def _():
        o_ref[...] = jnp.zeros_like(o_ref)

    @pl.when(used)
    def _():
        x = x_ref[...]
        gate = _dot(x, wg_ref[...])
        up = _dot(x, wu_ref[...])
        act = (gate * jax.nn.sigmoid(gate) * up).astype(BF16)
        o_ref[...] += _dot(act, wd_ref[...])


def _expert_tiles(ff):
    for tf in (1024, 512, 256, 128):
        if ff % tf == 0:
            return tf
    return ff


def _experts(xs, tile_expert, n_used, w_gate, w_up, w_down, tm):
    n_rows, dim = xs.shape
    ff = w_gate.shape[2]
    tf = _expert_tiles(ff)
    nj = ff // tf

    def jj(i, j, nu):
        return jnp.where(i < nu[0], j, nj - 1)

    return pl.pallas_call(
        _expert_kernel,
        grid_spec=pltpu.PrefetchScalarGridSpec(
            num_scalar_prefetch=2,
            grid=(n_rows // tm, nj),
            in_specs=[pl.BlockSpec((tm, dim), lambda i, j, te, nu: (i, 0)),
                      pl.BlockSpec((None, dim, tf), lambda i, j, te, nu: (te[i], 0, jj(i, j, nu))),
                      pl.BlockSpec((None, dim, tf), lambda i, j, te, nu: (te[i], 0, jj(i, j, nu))),
                      pl.BlockSpec((None, tf, dim), lambda i, j, te, nu: (te[i], jj(i, j, nu), 0))],
            out_specs=pl.BlockSpec((tm, dim), lambda i, j, te, nu: (i, 0)),
        ),
        out_shape=jax.ShapeDtypeStruct((n_rows, dim), F32),
        compiler_params=_cparams(("arbitrary", "arbitrary")),
        name="moe_experts",
    )(tile_expert, n_used, xs, w_gate.astype(BF16), w_up.astype(BF16), w_down.astype(BF16))


def _combine_kernel(slots_ref, h_ref, route_ref, g_ref, y_ref, o_ref, ya_ref, yb_ref, sem, *, rows):
    base = pl.program_id(0) * rows
    bufs = (ya_ref, yb_ref)

    def copy(t, k):
        return pltpu.make_async_copy(y_ref.at[pl.ds(slots_ref[TOP_K * (base + t) + k], 1)],
                                     bufs[k].at[pl.ds(t, 1)], sem)

    def issue(t, carry):
        for k in range(TOP_K):
            copy(t, k).start()
        return carry

    def drain(t, carry):
        for k in range(TOP_K):
            copy(t, k).wait()
        return carry

    lax.fori_loop(0, rows, issue, 0)
    lax.fori_loop(0, rows, drain, 0)
    route = route_ref[...]
    g1 = route[:, ROUTE_G1:ROUTE_G1 + 1]
    g2 = route[:, ROUTE_G2:ROUTE_G2 + 1]
    dim = h_ref.shape[1]
    ss = jnp.zeros((rows, 1), F32)
    for c in range(dim // LANES):
        cs = slice(c * LANES, (c + 1) * LANES)
        hc = h_ref[:, cs] + g1 * ya_ref[:, c, :] + g2 * yb_ref[:, c, :]
        o_ref[:, cs] = hc
        ss = ss + jnp.sum(hc * hc, axis=-1, keepdims=True)
    o_ref[...] = o_ref[...] * lax.rsqrt(ss * (1.0 / dim) + RMS_EPS) * g_ref[...]


def _combine(h2d, route, slots, y, g):
    t, dim = h2d.shape
    tile_shape = y.shape[1:]
    rows = min(t, 256)
    return pl.pallas_call(
        functools.partial(_combine_kernel, rows=rows),
        grid_spec=pltpu.PrefetchScalarGridSpec(
            num_scalar_prefetch=1,
            grid=(t // rows,),
            in_specs=[pl.BlockSpec((rows, dim), lambda i, s: (i, 0)),
                      pl.BlockSpec((rows, LANES), lambda i, s: (i, 0)),
                      pl.BlockSpec((1, dim), lambda i, s: (0, 0)),
                      pl.BlockSpec(memory_space=pl.ANY)],
            out_specs=pl.BlockSpec((rows, dim), lambda i, s: (i, 0)),
            scratch_shapes=[pltpu.VMEM((rows,) + tile_shape, F32), pltpu.VMEM((rows,) + tile_shape, F32),
                            pltpu.SemaphoreType.DMA(())],
        ),
        out_shape=jax.ShapeDtypeStruct((t, dim), F32),
        compiler_params=_cparams(("arbitrary",)),
        name="moe_combine",
    )(slots, h2d, route, g.reshape(1, dim), y)


def _moe_layer(h2d, norm_g, router_w, w_gate, w_up, w_down, final_g):
    t, dim = h2d.shape
    n_exp = router_w.shape[1]
    tm = min(t, 512)
    hn, route, counts = _router(h2d, norm_g, router_w)
    counts = counts[0, :n_exp].astype(jnp.int32)
    tiles = (counts + tm - 1) // tm
    tile_end = jnp.cumsum(tiles)
    row_start = (tile_end - tiles) * tm
    e12 = route[:, ROUTE_E1:ROUTE_E2 + 1].astype(jnp.int32)
    r12 = route[:, ROUTE_R1:ROUTE_R2 + 1].astype(jnp.int32)
    slots = (row_start[e12] + r12).reshape(-1)
    n_tiles = TOP_K * t // tm + n_exp
    n_used = tile_end[-1:]
    tile_ids = jnp.minimum(jnp.arange(n_tiles, dtype=jnp.int32), n_used[0] - 1)
    tile_expert = jnp.sum(tile_ids[:, None] >= tile_end[None, :], axis=1).astype(jnp.int32)
    n_rows = n_tiles * tm
    xs = _dispatch(hn.reshape(t, dim // LANES, LANES), slots, n_rows).reshape(n_rows, dim)
    y = _experts(xs, tile_expert, n_used, w_gate, w_up, w_down, tm)
    return _combine(h2d, route, slots, y.reshape(n_rows, dim // LANES, LANES), final_g)


def kernel(x, l0_norm_mix, l0_pool_w, l0_pool_scale, l0_norm_ffn, l0_ffn_w_gate, l0_ffn_w_up, l0_ffn_w_down, l1_norm_mix, l1_mix, l1_w_r, l1_w_k, l1_w_v, l1_w_o, l1_decay_w0, l1_decay_w1, l1_decay_w2, l1_iclr_a0, l1_iclr_a1, l1_iclr_a2, l1_gate_g1, l1_gate_g2, l1_k_k, l1_k_a, l1_r_k, l1_lnx_w, l1_lnx_b, l1_norm_ffn, l1_router, l1_moe_w_gate, l1_moe_w_up, l1_moe_w_down, final_norm):
    bsz, seq, dim = x.shape
    t = bsz * seq
    h = _pool_layer(x, l0_norm_mix, l0_pool_w, l0_pool_scale)
    h = _ffn_layer(h.reshape(t, dim), l0_norm_ffn, l0_ffn_w_gate, l0_ffn_w_up, l0_ffn_w_down)
    xr, xw, xk, xv, xa, xg = [m.reshape(t, dim) for m in _rwkv_prep(h.reshape(bsz, seq, dim), l1_norm_mix, l1_mix)]
    r = _proj(xr, l1_w_r)
    k, kk = _proj(xk, l1_w_k, kk_scale=l1_k_k)
    v = _proj(xv, l1_w_v)
    lw = _lora(xw, l1_decay_w1, l1_decay_w2, l1_decay_w0, "decay")
    a = _lora(xa, l1_iclr_a1, l1_iclr_a2, l1_iclr_a0, "iclr")
    g = _lora(xg, l1_gate_g1[None], l1_gate_g2[None], jnp.zeros((1, dim), F32), "gate")[0]
    b3 = lambda z: z.reshape(bsz, seq, dim)
    b4 = lambda z: z.reshape(2, bsz, seq, dim)
    at, rt, bb, kb, ui, yi, wt = _scan_intra(b3(r), b3(k), b3(v), b3(kk), b4(lw), b4(a), l1_k_a)
    y = _scan_seq(at, rt, bb, kb, ui, yi, b3(v), wt)
    h = _rwkv_out(y.reshape(2, t, dim), r, k, v, a, g, h, l1_k_a, l1_r_k, l1_lnx_w, l1_lnx_b, l1_w_o)
    out = _moe_layer(h, l1_norm_ffn, l1_router, l1_moe_w_gate, l1_moe_w_up, l1_moe_w_down, final_norm)
    return out.reshape(bsz, seq, dim)
```

```python
import functools

import jax
import jax.numpy as jnp
from jax import lax
from jax.experimental import pallas as pl
from jax.experimental.pallas import tpu as pltpu

F32 = jnp.float32
BF16 = jnp.bfloat16

HEAD_SIZE = 64
LANES = 128
POOL_WINDOWS = (2, 4, 8, 16)
POOL_HALO = 8
TOP_K = 2
RMS_EPS = 1e-6
LNX_EPS = 64e-5
CHUNK = 64
VMEM_LIMIT_BYTES = 56 * 1024 * 1024


def _cparams(sem):
    return pltpu.CompilerParams(dimension_semantics=sem, vmem_limit_bytes=VMEM_LIMIT_BYTES)


def _dot(a, b):
    return jnp.dot(a, b, preferred_element_type=F32)


def _dot_nt(a, b):
    return lax.dot_general(a, b, (((1,), (1,)), ((), ())), preferred_element_type=F32)


def _rms(x, g):
    return x * lax.rsqrt(jnp.mean(x * x, axis=-1, keepdims=True) + RMS_EPS) * g


def _split3(x):
    hi = x.astype(BF16)
    r1 = x - hi.astype(F32)
    mid = r1.astype(BF16)
    lo = (r1 - mid.astype(F32)).astype(BF16)
    return hi, mid, lo


def _head_ones():
    r = lax.broadcasted_iota(jnp.int32, (LANES, LANES), 0) // HEAD_SIZE
    c = lax.broadcasted_iota(jnp.int32, (LANES, LANES), 1) // HEAD_SIZE
    return (r == c).astype(BF16)


def _head_sum(x, ones_bd):
    hi, mid, lo = _split3(x)
    return _dot(hi, ones_bd) + _dot(mid, ones_bd) + _dot(lo, ones_bd)


def _pool_kernel(x_ref, xp_ref, xn_ref, g_ref, pw_ref, ps_ref, o_ref, ext_ref, *, seq, ts, cg):
    i = pl.program_id(1)
    nt = pl.num_programs(1)
    g = g_ref[...]
    x = x_ref[...]
    hn = _rms(x, g)
    has_prev = (i > 0).astype(F32)
    has_next = (i < nt - 1).astype(F32)
    ext_ref[0:POOL_HALO, :] = _rms(xp_ref[...], g) * has_prev
    ext_ref[POOL_HALO:POOL_HALO + ts, :] = hn
    ext_ref[POOL_HALO + ts:POOL_HALO + ts + POOL_HALO, :] = _rms(xn_ref[...], g) * has_next
    pos = i * ts + lax.broadcasted_iota(jnp.int32, (ts, 1), 0)
    for gi, w in enumerate(POOL_WINDOWS):
        cs = slice(gi * cg, (gi + 1) * cg)
        lo_off = -(w // 2)
        hi_off = w - w // 2 - 1
        tot = None
        for o in range(lo_off, hi_off + 1):
            part = ext_ref[POOL_HALO + o:POOL_HALO + o + ts, cs]
            tot = part if tot is None else tot + part
        lo = jnp.maximum(pos + lo_off, 0)
        hi = jnp.minimum(pos + hi_off + 1, seq)
        cnt = (hi - lo).astype(F32)
        pooled = tot / cnt - hn[:, cs]
        mixed = _dot(pooled.astype(BF16), pw_ref[gi])
        o_ref[:, cs] = x[:, cs] + mixed * ps_ref[:, cs]


def _pool_layer(x, g, pool_w, pool_scale):
    bsz, seq, dim = x.shape
    ng = len(POOL_WINDOWS)
    cg = dim // ng
    ts = min(seq, 512)
    nt = seq // ts
    hb = ts // POOL_HALO
    nhb = seq // POOL_HALO
    kern = functools.partial(_pool_kernel, seq=seq, ts=ts, cg=cg)
    return pl.pallas_call(
        kern,
        grid=(bsz, nt),
        in_specs=[
            pl.BlockSpec((None, ts, dim), lambda b, i: (b, i, 0)),
            pl.BlockSpec((None, POOL_HALO, dim), lambda b, i: (b, jnp.maximum(i * hb - 1, 0), 0)),
            pl.BlockSpec((None, POOL_HALO, dim), lambda b, i: (b, jnp.minimum((i + 1) * hb, nhb - 1), 0)),
            pl.BlockSpec((1, dim), lambda b, i: (0, 0)),
            pl.BlockSpec((ng, cg, cg), lambda b, i: (0, 0, 0)),
            pl.BlockSpec((1, dim), lambda b, i: (0, 0)),
        ],
        out_specs=pl.BlockSpec((None, ts, dim), lambda b, i: (b, i, 0)),
        out_shape=jax.ShapeDtypeStruct(x.shape, F32),
        scratch_shapes=[pltpu.VMEM((ts + 2 * POOL_HALO, dim), F32)],
        compiler_params=_cparams(("parallel", "arbitrary")),
        name="pool_layer",
    )(x, x, x, g.reshape(1, dim), pool_w.astype(BF16), pool_scale.reshape(1, dim))


def _ffn_kernel(x_ref, g_ref, wg_ref, wu_ref, wd_ref, o_ref, hn_ref):
    j = pl.program_id(1)

    @pl.when(j == 0)
    def _():
        x = x_ref[...]
        hn_ref[...] = _rms(x, g_ref[...]).astype(BF16)
        o_ref[...] = x

    hn = hn_ref[...]
    gate = _dot(hn, wg_ref[...])
    up = _dot(hn, wu_ref[...])
    act = (gate * jax.nn.sigmoid(gate) * up).astype(BF16)
    o_ref[...] += _dot(act, wd_ref[...])


def _ffn_layer(x2d, g, w_gate, w_up, w_down):
    t, dim = x2d.shape
    ff = w_gate.shape[1]
    tm = min(t, 512)
    tf = 512 if ff % 512 == 0 else ff
    return pl.pallas_call(
        _ffn_kernel,
        grid=(t // tm, ff // tf),
        in_specs=[
            pl.BlockSpec((tm, dim), lambda i, j: (i, 0)),
            pl.BlockSpec((1, dim), lambda i, j: (0, 0)),
            pl.BlockSpec((dim, tf), lambda i, j: (0, j)),
            pl.BlockSpec((dim, tf), lambda i, j: (0, j)),
            pl.BlockSpec((tf, dim), lambda i, j: (j, 0)),
        ],
        out_specs=pl.BlockSpec((tm, dim), lambda i, j: (i, 0)),
        out_shape=jax.ShapeDtypeStruct((t, dim), F32),
        scratch_shapes=[pltpu.VMEM((tm, dim), BF16)],
        compiler_params=_cparams(("parallel", "arbitrary")),
        name="ffn_dense",
    )(x2d, g.reshape(1, dim), w_gate.astype(BF16), w_up.astype(BF16), w_down.astype(BF16))


def _prep_kernel(x_ref, xp_ref, xn_ref, g_ref, mix_ref, *o_refs, ts):
    i = pl.program_id(1)
    nt = pl.num_programs(1)
    g = g_ref[...]
    hn = _rms(x_ref[...], g)
    prev_row = _rms(xp_ref[POOL_HALO - 1:POOL_HALO, :], g) * (i > 0).astype(F32)
    next_row = _rms(xn_ref[0:1, :], g) * (i < nt - 1).astype(F32)
    row = lax.broadcasted_iota(jnp.int32, (ts, 1), 0)
    prev = jnp.where(row == 0, prev_row, pltpu.roll(hn, 1, axis=0))
    nxt = jnp.where(row == ts - 1, next_row, pltpu.roll(hn, ts - 1, axis=0))
    xx = 0.5 * (prev + nxt) - hn
    for n, o_ref in enumerate(o_refs):
        o_ref[...] = (hn + xx * mix_ref[n:n + 1, :]).astype(BF16)


def _rwkv_prep(h, g, mix):
    bsz, seq, dim = h.shape
    nmix = mix.shape[0]
    ts = min(seq, 512)
    nt = seq // ts
    hb = ts // POOL_HALO
    nhb = seq // POOL_HALO
    spec = pl.BlockSpec((None, ts, dim), lambda b, i: (b, i, 0))
    return pl.pallas_call(
        functools.partial(_prep_kernel, ts=ts),
        grid=(bsz, nt),
        in_specs=[
            spec,
            pl.BlockSpec((None, POOL_HALO, dim), lambda b, i: (b, jnp.maximum(i * hb - 1, 0), 0)),
            pl.BlockSpec((None, POOL_HALO, dim), lambda b, i: (b, jnp.minimum((i + 1) * hb, nhb - 1), 0)),
            pl.BlockSpec((1, dim), lambda b, i: (0, 0)),
            pl.BlockSpec((nmix, dim), lambda b, i: (0, 0)),
        ],
        out_specs=[spec] * nmix,
        out_shape=[jax.ShapeDtypeStruct(h.shape, BF16)] * nmix,
        compiler_params=_cparams(("parallel", "arbitrary")),
        name="rwkv_prep",
    )(h, h, h, g.reshape(1, dim), mix)


def _mm_kernel(x_ref, w_ref, o_ref):
    o_ref[...] = _dot(x_ref[...], w_ref[...])


def _mm_key_kernel(x_ref, w_ref, kk_scale_ref, k_ref, kk_ref):
    k = _dot(x_ref[...], w_ref[...])
    k_ref[...] = k
    ones_bd = _head_ones()
    for c in range(k.shape[1] // LANES):
        cs = slice(c * LANES, (c + 1) * LANES)
        kq = k[:, cs] * kk_scale_ref[:, cs]
        ss = _head_sum(kq * kq, ones_bd)
        kk_ref[:, cs] = kq * lax.rsqrt(jnp.maximum(ss, 1e-24))


def _proj(x2d, w, kk_scale=None):
    t, kdim = x2d.shape
    n = w.shape[1]
    tm = min(t, 1024)
    tn = min(n, 1024)
    x_spec = pl.BlockSpec((tm, kdim), lambda i, j: (i, 0))
    w_spec = pl.BlockSpec((kdim, tn), lambda i, j: (0, j))
    o_spec = pl.BlockSpec((tm, tn), lambda i, j: (i, j))
    o_shape = jax.ShapeDtypeStruct((t, n), F32)
    if kk_scale is None:
        return pl.pallas_call(
            _mm_kernel, grid=(t // tm, n // tn), in_specs=[x_spec, w_spec], out_specs=o_spec,
            out_shape=o_shape, compiler_params=_cparams(("parallel", "arbitrary")), name="rwkv_proj",
        )(x2d, w.astype(BF16))
    return pl.pallas_call(
        _mm_key_kernel, grid=(t // tm, n // tn),
        in_specs=[x_spec, w_spec, pl.BlockSpec((1, tn), lambda i, j: (0, j))],
        out_specs=[o_spec, o_spec], out_shape=[o_shape, o_shape],
        compiler_params=_cparams(("parallel", "arbitrary")), name="rwkv_proj_key",
    )(x2d, w.astype(BF16), kk_scale.reshape(1, n))


def _lora_kernel(x_ref, w1_ref, w2_ref, b_ref, o_ref, *, mode, hid):
    h = _dot(x_ref[...], w1_ref[...])
    if mode == "decay":
        h = jnp.tanh(h)
    elif mode == "gate":
        h = jax.nn.sigmoid(h)
    h = h.astype(BF16)
    for e in range(o_ref.shape[0]):
        z = _dot(h[:, e * hid:(e + 1) * hid], w2_ref[e])
        if mode == "decay":
            z = -jax.nn.sigmoid(b_ref[e] + z) * jnp.exp(F32(-0.5))
        elif mode == "iclr":
            z = jax.nn.sigmoid(b_ref[e] + z)
        o_ref[e] = z


def _lora(x2d, w1, w2, bias, mode):
    t, dim = x2d.shape
    ne, _, rank = w1.shape
    hid = -(-rank // LANES) * LANES
    w1p = jnp.pad(w1, ((0, 0), (0, 0), (0, hid - rank)))
    w1p = jnp.transpose(w1p, (1, 0, 2)).reshape(dim, ne * hid).astype(BF16)
    w2p = jnp.pad(w2, ((0, 0), (0, hid - rank), (0, 0))).astype(BF16)
    tm = min(t, 512)
    return pl.pallas_call(
        functools.partial(_lora_kernel, mode=mode, hid=hid),
        grid=(t // tm,),
        in_specs=[
            pl.BlockSpec((tm, dim), lambda i: (i, 0)),
            pl.BlockSpec((dim, ne * hid), lambda i: (0, 0)),
            pl.BlockSpec((ne, hid, dim), lambda i: (0, 0, 0)),
            pl.BlockSpec((ne, 1, dim), lambda i: (0, 0, 0)),
        ],
        out_specs=pl.BlockSpec((ne, tm, dim), lambda i: (0, i, 0)),
        out_shape=jax.ShapeDtypeStruct((ne, t, dim), F32),
        compiler_params=_cparams(("parallel",)),
        name="rwkv_lora_" + mode,
    )(x2d, w1p, w2p, bias.reshape(ne, 1, dim))


def _pair_blockdiag(x):
    lane = lax.broadcasted_iota(jnp.int32, x.shape, 1)
    top = jnp.where(lane < HEAD_SIZE, x, 0.0)
    bot = jnp.where(lane >= HEAD_SIZE, x, 0.0)
    return jnp.concatenate([top, bot], axis=0).astype(BF16)


def _pair_dot(a, b):
    return _dot(a.astype(BF16), _pair_blockdiag(b))


def _scan_intra_kernel(r_ref, k_ref, v_ref, kk_ref, lw_ref, a_ref, ka_ref,
                       at_ref, rt_ref, bb_ref, kb_ref, ui_ref, yi_ref, wt_ref):
    d = pl.program_id(0)
    sgn = 1 - 2 * d
    fwd_f = (1 - d).astype(F32)
    L = CHUNK
    t_sq = lax.broadcasted_iota(jnp.int32, (L, L), 0)
    s_sq = lax.broadcasted_iota(jnp.int32, (L, L), 1)
    tri = (((t_sq - s_sq) * sgn) >= 0).astype(BF16)
    t_p = lax.broadcasted_iota(jnp.int32, (L, LANES), 0)
    s_p = lax.broadcasted_iota(jnp.int32, (L, LANES), 1) % HEAD_SIZE
    rel = (t_p - s_p) * sgn
    m_strict = rel > 0
    m_read = (rel + (1 - d)) > 0
    eye = (t_p == s_p).astype(F32)
    cols = [slice(p * LANES, (p + 1) * LANES) for p in range(r_ref.shape[1] // LANES)]

    def each(fn, *lists):
        return [fn(*args) for args in zip(*lists)]

    def cumulative(lw):
        hi, mid, lo = _split3(lw)
        return _dot(tri, hi) + _dot(tri, mid) + _dot(tri, lo)

    lw = [lw_ref[:, cs] for cs in cols]
    c_incl = each(cumulative, lw)
    c_excl = each(lambda c, w: c - w, c_incl, lw)
    tot = each(lambda w: jnp.sum(w, axis=0, keepdims=True), lw)
    kk = [kk_ref[:, cs] for cs in cols]
    a = [a_ref[:, cs] for cs in cols]
    beta = each(lambda x, y: x * y, kk, a)
    kt = [k_ref[:, cs] * (1.0 + (a_p - 1.0) * ka_ref[:, cs]) for cs, a_p in zip(cols, a)]
    ah = each(lambda x, c: -x * jnp.exp(c), kk, c_excl)
    rh = [r_ref[:, cs] * jnp.exp(c + fwd_f * w) for cs, c, w in zip(cols, c_excl, lw)]
    e_neg = each(lambda c: jnp.exp(-c), c_incl)
    lhs = each(lambda x, y: jnp.concatenate([x, y], axis=0).astype(BF16), ah, rh)
    a_b = each(lambda l, b, e: _dot_nt(l, _pair_blockdiag(b * e)), lhs, beta, e_neg)
    a_k = each(lambda l, b, e: _dot_nt(l, _pair_blockdiag(b * e)), lhs, kt, e_neg)
    n_ab = each(lambda m: jnp.where(m_strict, m[:L], 0.0), a_b)
    n_ak = each(lambda m: jnp.where(m_strict, m[:L], 0.0), a_k)
    n_rb = each(lambda m: jnp.where(m_read, m[L:], 0.0), a_b)
    n_rk = each(lambda m: jnp.where(m_read, m[L:], 0.0), a_k)
    v = [v_ref[:, cs] for cs in cols]
    x_ak = each(_pair_dot, n_ak, v)
    y_rk = each(_pair_dot, n_rk, v)
    tm = each(lambda n: eye + n, n_ab)
    pw = n_ab
    span = 1
    while span * 2 < L:
        pw = each(_pair_dot, pw, pw)
        tm = each(lambda t, q: t + _pair_dot(q, t), tm, pw)
        span *= 2
    at = each(_pair_dot, tm, ah)
    ui = each(_pair_dot, tm, x_ak)
    rt = each(lambda x, n, y: x + _pair_dot(n, y), rh, n_rb, at)
    yi = each(lambda n, u, y: _pair_dot(n, u) + y, n_rb, ui, y_rk)
    for p, cs in enumerate(cols):
        e_bar = jnp.exp(tot[p] - c_incl[p])
        at_ref[:, cs] = at[p].astype(BF16)
        rt_ref[:, cs] = rt[p].astype(BF16)
        bb_ref[:, cs] = (beta[p] * e_bar).astype(BF16)
        kb_ref[:, cs] = (kt[p] * e_bar).astype(BF16)
        ui_ref[:, cs] = ui[p]
        yi_ref[:, cs] = yi[p]
        wt_ref[:, cs] = jnp.exp(tot[p])


def _scan_intra(r, k, v, kk, lw, a, k_a):
    bsz, seq, dim = r.shape
    nc = seq // CHUNK
    cw = min(dim, 2048)
    nq = dim // cw
    grid = (2, bsz, nc, nq)
    tok = pl.BlockSpec((None, CHUNK, cw), lambda d, b, c, q: (b, c, q))
    dirtok = pl.BlockSpec((None, None, CHUNK, cw), lambda d, b, c, q: (d, b, c, q))
    wt_spec = pl.BlockSpec((None, None, None, 1, cw), lambda d, b, c, q: (d, b, c, 0, q))
    big = lambda dt: jax.ShapeDtypeStruct((2, bsz, seq, dim), dt)
    return pl.pallas_call(
        _scan_intra_kernel,
        grid=grid,
        in_specs=[tok, tok, tok, tok, dirtok, dirtok, pl.BlockSpec((1, cw), lambda d, b, c, q: (0, q))],
        out_specs=[dirtok, dirtok, dirtok, dirtok, dirtok, dirtok, wt_spec],
        out_shape=[big(BF16), big(BF16), big(BF16), big(BF16), big(F32), big(F32),
                   jax.ShapeDtypeStruct((2, bsz, nc, 1, dim), F32)],
        compiler_params=_cparams(("parallel", "parallel", "parallel", "arbitrary")),
        name="wkv_intra",
    )(r, k, v, kk, lw, a, k_a.reshape(1, dim))


def _scan_seq_kernel(at_ref, rt_ref, bb_ref, kb_ref, ui_ref, yi_ref, v_ref, wt_ref, y_ref, s_ref):
    c = pl.program_id(2)
    L = CHUNK

    @pl.when(c == 0)
    def _():
        s_ref[...] = jnp.zeros_like(s_ref)

    row_h = lax.broadcasted_iota(jnp.int32, (LANES, LANES), 0) // HEAD_SIZE
    col_h = lax.broadcasted_iota(jnp.int32, (LANES, LANES), 1) // HEAD_SIZE
    same_head = row_h == col_h
    pairs = range(s_ref.shape[0])
    cols = [slice(p * LANES, (p + 1) * LANES) for p in pairs]
    s = [s_ref[p] for p in pairs]
    uy = [_dot_nt(jnp.concatenate([at_ref[:, cs], rt_ref[:, cs]], axis=0), s_p.astype(BF16))
          for cs, s_p in zip(cols, s)]
    for cs, uy_p in zip(cols, uy):
        y_ref[:, cs] = uy_p[L:] + yi_ref[:, cs]
    uv_t = [jnp.concatenate([uy_p[:L] + ui_ref[:, cs], v_ref[:, cs]], axis=0).T.astype(BF16)
            for cs, uy_p in zip(cols, uy)]
    upd = [_dot(uv_p, jnp.concatenate([bb_ref[:, cs], kb_ref[:, cs]], axis=0))
           for cs, uv_p in zip(cols, uv_t)]
    for p, cs in enumerate(cols):
        s_ref[p] = s[p] * wt_ref[:, cs] + jnp.where(same_head, upd[p], 0.0)


def _scan_seq(at, rt, bb, kb, ui, yi, v, wt):
    _, bsz, seq, dim = at.shape
    nc = seq // CHUNK

    def cidx(d, c):
        return c + d * (nc - 1 - 2 * c)

    dirtok = pl.BlockSpec((None, None, CHUNK, dim), lambda d, b, c: (d, b, cidx(d, c), 0))
    return pl.pallas_call(
        _scan_seq_kernel,
        grid=(2, bsz, nc),
        in_specs=[dirtok, dirtok, dirtok, dirtok, dirtok, dirtok,
                  pl.BlockSpec((None, CHUNK, dim), lambda d, b, c: (b, cidx(d, c), 0)),
                  pl.BlockSpec((None, None, None, 1, dim), lambda d, b, c: (d, b, cidx(d, c), 0, 0))],
        out_specs=dirtok,
        out_shape=jax.ShapeDtypeStruct((2, bsz, seq, dim), F32),
        scratch_shapes=[pltpu.VMEM((dim // LANES, LANES, LANES), F32)],
        compiler_params=_cparams(("parallel", "parallel", "arbitrary")),
        name="wkv_seq",
    )(at, rt, bb, kb, ui, yi, v, wt)


def _rwkv_out_kernel(yf_ref, yb_ref, r_ref, k_ref, v_ref, a_ref, g_ref, h_ref,
                     ka_ref, rk_ref, lw_ref, lb_ref, wo_ref, o_ref, z_ref):
    ones_bd = _head_ones()
    inv_n = 1.0 / HEAD_SIZE
    for c in range(h_ref.shape[1] // LANES):
        cs = slice(c * LANES, (c + 1) * LANES)
        y = yf_ref[:, cs] + yb_ref[:, cs]
        mu = _head_sum(y, ones_bd) * inv_n
        yc = y - mu
        var = _head_sum(yc * yc, ones_bd) * inv_n
        yn = yc * lax.rsqrt(var + LNX_EPS) * lw_ref[:, cs] + lb_ref[:, cs]
        kd = k_ref[:, cs] * (1.0 + (a_ref[:, cs] - 1.0) * ka_ref[:, cs])
        bonus = _head_sum(r_ref[:, cs] * kd * rk_ref[:, cs], ones_bd) * v_ref[:, cs]
        z_ref[:, cs] = ((yn + bonus) * g_ref[:, cs]).astype(BF16)
    o_ref[...] = h_ref[...] + _dot(z_ref[...], wo_ref[...])


def _rwkv_out(y, r, k, v, a, g, h2d, k_a, r_k, lnx_w, lnx_b, w_o):
    t, dim = h2d.shape
    tm = min(t, 256)
    tok = pl.BlockSpec((tm, dim), lambda i: (i, 0))
    vec = pl.BlockSpec((1, dim), lambda i: (0, 0))
    return pl.pallas_call(
        _rwkv_out_kernel,
        grid=(t // tm,),
        in_specs=[pl.BlockSpec((None, tm, dim), lambda i: (0, i, 0)),
                  pl.BlockSpec((None, tm, dim), lambda i: (1, i, 0)),
                  tok, tok, tok,
                  pl.BlockSpec((None, tm, dim), lambda i: (0, i, 0)),
                  tok, tok, vec, vec, vec, vec,
                  pl.BlockSpec((dim, dim), lambda i: (0, 0))],
        out_specs=tok,
        out_shape=jax.ShapeDtypeStruct((t, dim), F32),
        scratch_shapes=[pltpu.VMEM((tm, dim), BF16)],
        compiler_params=_cparams(("parallel",)),
        name="rwkv_out",
    )(y, y, r, k, v, a, g, h2d, k_a.reshape(1, dim), r_k.reshape(1, dim),
      lnx_w.reshape(1, dim), lnx_b.reshape(1, dim), w_o.astype(BF16))


ROUTE_E1, ROUTE_E2, ROUTE_G1, ROUTE_G2, ROUTE_R1, ROUTE_R2 = range(6)


def _router_kernel(x_ref, g_ref, w_ref, hn_ref, route_ref, cnt_ref, *, n_exp, tm):
    i = pl.program_id(0)

    @pl.when(i == 0)
    def _():
        cnt_ref[...] = jnp.zeros_like(cnt_ref)

    hn = _rms(x_ref[...], g_ref[...])
    for c in range(hn_ref.shape[1]):
        hn_ref[:, c, :] = hn[:, c * LANES:(c + 1) * LANES]
    x_hi, x_mid, x_lo = _split3(hn)
    w_hi, w_mid, w_lo = w_ref[0], w_ref[1], w_ref[2]
    logits = (_dot(x_hi, w_hi) + _dot(x_hi, w_mid) + _dot(x_mid, w_hi)
              + _dot(x_hi, w_lo) + _dot(x_mid, w_mid) + _dot(x_lo, w_hi))
    lane = lax.broadcasted_iota(jnp.int32, (tm, LANES), 1)
    neg = F32(-jnp.inf)
    lg = jnp.where(lane < n_exp, logits, neg)
    m1 = jnp.max(lg, axis=-1, keepdims=True)
    i1 = jnp.min(jnp.where(lg == m1, lane, LANES), axis=-1, keepdims=True)
    lg2 = jnp.where(lane == i1, neg, lg)
    m2 = jnp.max(lg2, axis=-1, keepdims=True)
    i2 = jnp.min(jnp.where(lg2 == m2, lane, LANES), axis=-1, keepdims=True)
    e2 = jnp.exp(m2 - m1)
    den = 1.0 + e2
    g1 = 1.0 / den
    g2 = e2 / den
    oh1 = lane == i1
    oh2 = lane == i2
    oh = oh1.astype(F32) + oh2.astype(F32)
    rr = lax.broadcasted_iota(jnp.int32, (tm, tm), 0)
    cc = lax.broadcasted_iota(jnp.int32, (tm, tm), 1)
    before = (cc < rr).astype(BF16)
    prefix = _dot(before, oh.astype(BF16)) + cnt_ref[...]
    r1 = jnp.sum(jnp.where(oh1, prefix, 0.0), axis=-1, keepdims=True)
    r2 = jnp.sum(jnp.where(oh2, prefix, 0.0), axis=-1, keepdims=True)
    cnt_ref[...] += jnp.sum(oh, axis=0, keepdims=True)
    route = jnp.zeros((tm, LANES), F32)
    for col, val in ((ROUTE_E1, i1.astype(F32)), (ROUTE_E2, i2.astype(F32)), (ROUTE_G1, g1),
                     (ROUTE_G2, g2), (ROUTE_R1, r1), (ROUTE_R2, r2)):
        route = jnp.where(lane == col, val, route)
    route_ref[...] = route


def _router(h2d, g, router_w):
    t, dim = h2d.shape
    n_exp = router_w.shape[1]
    tm = min(t, 512)
    wp = jnp.pad(router_w, ((0, 0), (0, LANES - n_exp)))
    w3 = jnp.stack(_split3(wp))
    return pl.pallas_call(
        functools.partial(_router_kernel, n_exp=n_exp, tm=tm),
        grid=(t // tm,),
        in_specs=[pl.BlockSpec((tm, dim), lambda i: (i, 0)),
                  pl.BlockSpec((1, dim), lambda i: (0, 0)),
                  pl.BlockSpec((3, dim, LANES), lambda i: (0, 0, 0))],
        out_specs=[pl.BlockSpec((tm, dim // LANES, LANES), lambda i: (i, 0, 0)),
                   pl.BlockSpec((tm, LANES), lambda i: (i, 0)),
                   pl.BlockSpec((1, LANES), lambda i: (0, 0))],
        out_shape=[jax.ShapeDtypeStruct((t, dim // LANES, LANES), F32),
                   jax.ShapeDtypeStruct((t, LANES), F32),
                   jax.ShapeDtypeStruct((1, LANES), F32)],
        compiler_params=_cparams(("arbitrary",)),
        name="moe_router",
    )(h2d, g.reshape(1, dim), w3)


def _dispatch_kernel(src_ref, hn_ref, xs_ref, buf_ref, sem, *, rows):
    base = pl.program_id(0) * rows

    def copy(t):
        return pltpu.make_async_copy(hn_ref.at[pl.ds(src_ref[base + t], 1)], buf_ref.at[pl.ds(t, 1)], sem)

    def issue(t, carry):
        copy(t).start()
        return carry

    def drain(t, carry):
        copy(t).wait()
        return carry

    lax.fori_loop(0, rows, issue, 0)
    lax.fori_loop(0, rows, drain, 0)
    for c in range(buf_ref.shape[1]):
        xs_ref[:, c * LANES:(c + 1) * LANES] = buf_ref[:, c, :].astype(BF16)


def _dispatch(hn, src_rows):
    tile_shape = hn.shape[1:]
    dim = tile_shape[0] * tile_shape[1]
    n_rows = src_rows.shape[0]
    rows = min(n_rows, 256)
    return pl.pallas_call(
        functools.partial(_dispatch_kernel, rows=rows),
        grid_spec=pltpu.PrefetchScalarGridSpec(
            num_scalar_prefetch=1,
            grid=(n_rows // rows,),
            in_specs=[pl.BlockSpec(memory_space=pl.ANY)],
            out_specs=pl.BlockSpec((rows, dim), lambda i, s: (i, 0)),
            scratch_shapes=[pltpu.VMEM((rows,) + tile_shape, F32), pltpu.SemaphoreType.DMA(())],
        ),
        out_shape=jax.ShapeDtypeStruct((n_rows, dim), BF16),
        compiler_params=_cparams(("arbitrary",)),
        name="moe_dispatch",
    )(src_rows, hn)


def _expert_kernel(te_ref, nu_ref, x_ref, wg_ref, wu_ref, wd_ref, o_ref, acc_ref):
    del te_ref
    i = pl.program_id(0)
    j = pl.program_id(1)
    used = i < nu_ref[0]

    @pl.when(j == 0)
    def _():
        acc_ref[...] = jnp.zeros_like(acc_ref)

    @pl.when(used)
    def _():
        x = x_ref[...]
        gate = _dot(x, wg_ref[...])
        up = _dot(x, wu_ref[...])
        act = (gate * jax.nn.sigmoid(gate) * up).astype(BF16)
        acc_ref[...] += _dot(act, wd_ref[...])

    @pl.when(j == pl.num_programs(1) - 1)
    def _():
        for c in range(o_ref.shape[1]):
            o_ref[:, c, :] = acc_ref[:, c * LANES:(c + 1) * LANES]


def _expert_tiles(ff):
    for tf in (1024, 512, 256, 128):
        if ff % tf == 0:
            return tf
    return ff


def _experts(xs, tile_expert, n_used, w_gate, w_up, w_down, tm):
    n_rows, dim = xs.shape
    ff = w_gate.shape[2]
    tf = _expert_tiles(ff)
    nj = ff // tf

    def jj(i, j, nu):
        return jnp.where(i < nu[0], j, nj - 1)

    return pl.pallas_call(
        _expert_kernel,
        grid_spec=pltpu.PrefetchScalarGridSpec(
            num_scalar_prefetch=2,
            grid=(n_rows // tm, nj),
            in_specs=[pl.BlockSpec((tm, dim), lambda i, j, te, nu: (i, 0)),
                      pl.BlockSpec((None, dim, tf), lambda i, j, te, nu: (te[i], 0, jj(i, j, nu))),
                      pl.BlockSpec((None, dim, tf), lambda i, j, te, nu: (te[i], 0, jj(i, j, nu))),
                      pl.BlockSpec((None, tf, dim), lambda i, j, te, nu: (te[i], jj(i, j, nu), 0))],
            out_specs=pl.BlockSpec((tm, dim // LANES, LANES), lambda i, j, te, nu: (i, 0, 0)),
            scratch_shapes=[pltpu.VMEM((tm, dim), F32)],
        ),
        out_shape=jax.ShapeDtypeStruct((n_rows, dim // LANES, LANES), F32),
        compiler_params=_cparams(("arbitrary", "arbitrary")),
        name="moe_experts",
    )(tile_expert, n_used, xs, w_gate.astype(BF16), w_up.astype(BF16), w_down.astype(BF16))


def _combine_kernel(slots_ref, h_ref, route_ref, g_ref, y_ref, o_ref, ya_ref, yb_ref, sem, *, rows):
    base = pl.program_id(0) * rows
    bufs = (ya_ref, yb_ref)

    def copy(t, k):
        return pltpu.make_async_copy(y_ref.at[pl.ds(slots_ref[TOP_K * (base + t) + k], 1)],
                                     bufs[k].at[pl.ds(t, 1)], sem)

    def issue(t, carry):
        for k in range(TOP_K):
            copy(t, k).start()
        return carry

    def drain(t, carry):
        for k in range(TOP_K):
            copy(t, k).wait()
        return carry

    lax.fori_loop(0, rows, issue, 0)
    lax.fori_loop(0, rows, drain, 0)
    route = route_ref[...]
    g1 = route[:, ROUTE_G1:ROUTE_G1 + 1]
    g2 = route[:, ROUTE_G2:ROUTE_G2 + 1]
    dim = h_ref.shape[1]
    ss = jnp.zeros((rows, 1), F32)
    for c in range(dim // LANES):
        cs = slice(c * LANES, (c + 1) * LANES)
        hc = h_ref[:, cs] + g1 * ya_ref[:, c, :] + g2 * yb_ref[:, c, :]
        o_ref[:, cs] = hc
        ss = ss + jnp.sum(hc * hc, axis=-1, keepdims=True)
    o_ref[...] = o_ref[...] * lax.rsqrt(ss * (1.0 / dim) + RMS_EPS) * g_ref[...]


def _combine(h2d, route, slots, y, g):
    t, dim = h2d.shape
    tile_shape = y.shape[1:]
    rows = min(t, 256)
    return pl.pallas_call(
        functools.partial(_combine_kernel, rows=rows),
        grid_spec=pltpu.PrefetchScalarGridSpec(
            num_scalar_prefetch=1,
            grid=(t // rows,),
            in_specs=[pl.BlockSpec((rows, dim), lambda i, s: (i, 0)),
                      pl.BlockSpec((rows, LANES), lambda i, s: (i, 0)),
                      pl.BlockSpec((1, dim), lambda i, s: (0, 0)),
                      pl.BlockSpec(memory_space=pl.ANY)],
            out_specs=pl.BlockSpec((rows, dim), lambda i, s: (i, 0)),
            scratch_shapes=[pltpu.VMEM((rows,) + tile_shape, F32), pltpu.VMEM((rows,) + tile_shape, F32),
                            pltpu.SemaphoreType.DMA(())],
        ),
        out_shape=jax.ShapeDtypeStruct((t, dim), F32),
        compiler_params=_cparams(("arbitrary",)),
        name="moe_combine",
    )(slots, h2d, route, g.reshape(1, dim), y)


def _moe_layer(h2d, norm_g, router_w, w_gate, w_up, w_down, final_g):
    t, dim = h2d.shape
    n_exp = router_w.shape[1]
    tm = min(t, 512)
    hn, route, counts = _router(h2d, norm_g, router_w)
    counts = counts[0, :n_exp].astype(jnp.int32)
    tiles = (counts + tm - 1) // tm
    tile_end = jnp.cumsum(tiles)
    row_start = (tile_end - tiles) * tm
    e12 = route[:, ROUTE_E1:ROUTE_E2 + 1].astype(jnp.int32)
    r12 = route[:, ROUTE_R1:ROUTE_R2 + 1].astype(jnp.int32)
    slots = (row_start[e12] + r12).reshape(-1)
    n_tiles = TOP_K * t // tm + n_exp
    n_used = tile_end[-1:]
    tile_ids = jnp.minimum(jnp.arange(n_tiles, dtype=jnp.int32), n_used[0] - 1)
    tile_expert = jnp.sum(tile_ids[:, None] >= tile_end[None, :], axis=1).astype(jnp.int32)
    token = jnp.repeat(jnp.arange(t, dtype=jnp.int32), TOP_K)
    src_rows = jnp.zeros((n_tiles * tm,), jnp.int32).at[slots].set(token)
    xs = _dispatch(hn, src_rows)
    y = _experts(xs, tile_expert, n_used, w_gate, w_up, w_down, tm)
    return _combine(h2d, route, slots, y, final_g)


def kernel(x, l0_norm_mix, l0_pool_w, l0_pool_scale, l0_norm_ffn, l0_ffn_w_gate, l0_ffn_w_up, l0_ffn_w_down, l1_norm_mix, l1_mix, l1_w_r, l1_w_k, l1_w_v, l1_w_o, l1_decay_w0, l1_decay_w1, l1_decay_w2, l1_iclr_a0, l1_iclr_a1, l1_iclr_a2, l1_gate_g1, l1_gate_g2, l1_k_k, l1_k_a, l1_r_k, l1_lnx_w, l1_lnx_b, l1_norm_ffn, l1_router, l1_moe_w_gate, l1_moe_w_up, l1_moe_w_down, final_norm):
    bsz, seq, dim = x.shape
    t = bsz * seq
    h = _pool_layer(x, l0_norm_mix, l0_pool_w, l0_pool_scale)
    h = _ffn_layer(h.reshape(t, dim), l0_norm_ffn, l0_ffn_w_gate, l0_ffn_w_up, l0_ffn_w_down)
    xr, xw, xk, xv, xa, xg = [m.reshape(t, dim) for m in _rwkv_prep(h.reshape(bsz, seq, dim), l1_norm_mix, l1_mix)]
    r = _proj(xr, l1_w_r)
    k, kk = _proj(xk, l1_w_k, kk_scale=l1_k_k)
    v = _proj(xv, l1_w_v)
    lw = _lora(xw, l1_decay_w1, l1_decay_w2, l1_decay_w0, "decay")
    a = _lora(xa, l1_iclr_a1, l1_iclr_a2, l1_iclr_a0, "iclr")
    g = _lora(xg, l1_gate_g1[None], l1_gate_g2[None], jnp.zeros((1, dim), F32), "gate")[0]
    b3 = lambda z: z.reshape(bsz, seq, dim)
    b4 = lambda z: z.reshape(2, bsz, seq, dim)
    at, rt, bb, kb, ui, yi, wt = _scan_intra(b3(r), b3(k), b3(v), b3(kk), b4(lw), b4(a), l1_k_a)
    y = _scan_seq(at, rt, bb, kb, ui, yi, b3(v), wt)
    h = _rwkv_out(y.reshape(2, t, dim), r, k, v, a, g, h, l1_k_a, l1_r_k, l1_lnx_w, l1_lnx_b, l1_w_o)
    out = _moe_layer(h, l1_norm_ffn, l1_router, l1_moe_w_gate, l1_moe_w_up, l1_moe_w_down, final_norm)
    return out.reshape(bsz, seq, dim)
```

```python
import functools

import jax
import jax.numpy as jnp
from jax import lax
from jax.experimental import pallas as pl
from jax.experimental.pallas import tpu as pltpu

F32 = jnp.float32
BF16 = jnp.bfloat16

HEAD_SIZE = 64
LANES = 128
POOL_WINDOWS = (2, 4, 8, 16)
POOL_HALO = 8
TOP_K = 2
RMS_EPS = 1e-6
LNX_EPS = 64e-5
CHUNK = 64
VMEM_LIMIT_BYTES = 56 * 1024 * 1024


def _cparams(sem):
    return pltpu.CompilerParams(dimension_semantics=sem, vmem_limit_bytes=VMEM_LIMIT_BYTES)


def _dot(a, b):
    return jnp.dot(a, b, preferred_element_type=F32)


def _dot_nt(a, b):
    return lax.dot_general(a, b, (((1,), (1,)), ((), ())), preferred_element_type=F32)


def _rms(x, g):
    return x * lax.rsqrt(jnp.mean(x * x, axis=-1, keepdims=True) + RMS_EPS) * g


def _split3(x):
    hi = x.astype(BF16)
    r1 = x - hi.astype(F32)
    mid = r1.astype(BF16)
    lo = (r1 - mid.astype(F32)).astype(BF16)
    return hi, mid, lo


def _head_ones():
    r = lax.broadcasted_iota(jnp.int32, (LANES, LANES), 0) // HEAD_SIZE
    c = lax.broadcasted_iota(jnp.int32, (LANES, LANES), 1) // HEAD_SIZE
    return (r == c).astype(BF16)


def _head_sum(x, ones_bd):
    hi, mid, _ = _split3(x)
    return _dot(hi, ones_bd) + _dot(mid, ones_bd)


def _pool_kernel(x_ref, xp_ref, xn_ref, g_ref, pw_ref, ps_ref, o_ref, ext_ref, *, seq, ts, cg):
    i = pl.program_id(1)
    nt = pl.num_programs(1)
    g = g_ref[...]
    x = x_ref[...]
    hn = _rms(x, g)
    has_prev = (i > 0).astype(F32)
    has_next = (i < nt - 1).astype(F32)
    ext_ref[0:POOL_HALO, :] = _rms(xp_ref[...], g) * has_prev
    ext_ref[POOL_HALO:POOL_HALO + ts, :] = hn
    ext_ref[POOL_HALO + ts:POOL_HALO + ts + POOL_HALO, :] = _rms(xn_ref[...], g) * has_next
    pos = i * ts + lax.broadcasted_iota(jnp.int32, (ts, 1), 0)
    for gi, w in enumerate(POOL_WINDOWS):
        cs = slice(gi * cg, (gi + 1) * cg)
        lo_off = -(w // 2)
        hi_off = w - w // 2 - 1
        tot = None
        for o in range(lo_off, hi_off + 1):
            part = ext_ref[POOL_HALO + o:POOL_HALO + o + ts, cs]
            tot = part if tot is None else tot + part
        lo = jnp.maximum(pos + lo_off, 0)
        hi = jnp.minimum(pos + hi_off + 1, seq)
        cnt = (hi - lo).astype(F32)
        pooled = tot / cnt - hn[:, cs]
        mixed = _dot(pooled.astype(BF16), pw_ref[gi])
        o_ref[:, cs] = x[:, cs] + mixed * ps_ref[:, cs]


def _pool_layer(x, g, pool_w, pool_scale):
    bsz, seq, dim = x.shape
    ng = len(POOL_WINDOWS)
    cg = dim // ng
    ts = min(seq, 512)
    nt = seq // ts
    hb = ts // POOL_HALO
    nhb = seq // POOL_HALO
    kern = functools.partial(_pool_kernel, seq=seq, ts=ts, cg=cg)
    return pl.pallas_call(
        kern,
        grid=(bsz, nt),
        in_specs=[
            pl.BlockSpec((None, ts, dim), lambda b, i: (b, i, 0)),
            pl.BlockSpec((None, POOL_HALO, dim), lambda b, i: (b, jnp.maximum(i * hb - 1, 0), 0)),
            pl.BlockSpec((None, POOL_HALO, dim), lambda b, i: (b, jnp.minimum((i + 1) * hb, nhb - 1), 0)),
            pl.BlockSpec((1, dim), lambda b, i: (0, 0)),
            pl.BlockSpec((ng, cg, cg), lambda b, i: (0, 0, 0)),
            pl.BlockSpec((1, dim), lambda b, i: (0, 0)),
        ],
        out_specs=pl.BlockSpec((None, ts, dim), lambda b, i: (b, i, 0)),
        out_shape=jax.ShapeDtypeStruct(x.shape, F32),
        scratch_shapes=[pltpu.VMEM((ts + 2 * POOL_HALO, dim), F32)],
        compiler_params=_cparams(("parallel", "arbitrary")),
        name="pool_layer",
    )(x, x, x, g.reshape(1, dim), pool_w.astype(BF16), pool_scale.reshape(1, dim))


def _ffn_kernel(x_ref, g_ref, wg_ref, wu_ref, wd_ref, o_ref, hn_ref):
    j = pl.program_id(1)

    @pl.when(j == 0)
    def _():
        x = x_ref[...]
        hn_ref[...] = _rms(x, g_ref[...]).astype(BF16)
        o_ref[...] = x

    hn = hn_ref[...]
    gate = _dot(hn, wg_ref[...])
    up = _dot(hn, wu_ref[...])
    act = (gate * jax.nn.sigmoid(gate) * up).astype(BF16)
    o_ref[...] += _dot(act, wd_ref[...])


def _ffn_layer(x2d, g, w_gate, w_up, w_down):
    t, dim = x2d.shape
    ff = w_gate.shape[1]
    tm = min(t, 1024)
    tf = 512 if ff % 512 == 0 else ff
    return pl.pallas_call(
        _ffn_kernel,
        grid=(t // tm, ff // tf),
        in_specs=[
            pl.BlockSpec((tm, dim), lambda i, j: (i, 0), pipeline_mode=pl.Buffered(1)),
            pl.BlockSpec((1, dim), lambda i, j: (0, 0)),
            pl.BlockSpec((dim, tf), lambda i, j: (0, j)),
            pl.BlockSpec((dim, tf), lambda i, j: (0, j)),
            pl.BlockSpec((tf, dim), lambda i, j: (j, 0)),
        ],
        out_specs=pl.BlockSpec((tm, dim), lambda i, j: (i, 0)),
        out_shape=jax.ShapeDtypeStruct((t, dim), F32),
        scratch_shapes=[pltpu.VMEM((tm, dim), BF16)],
        compiler_params=_cparams(("parallel", "arbitrary")),
        name="ffn_dense",
    )(x2d, g.reshape(1, dim), w_gate.astype(BF16), w_up.astype(BF16), w_down.astype(BF16))


def _prep_kernel(x_ref, xp_ref, xn_ref, g_ref, mix_ref, *o_refs, ts):
    i = pl.program_id(1)
    nt = pl.num_programs(1)
    g = g_ref[...]
    hn = _rms(x_ref[...], g)
    prev_row = _rms(xp_ref[POOL_HALO - 1:POOL_HALO, :], g) * (i > 0).astype(F32)
    next_row = _rms(xn_ref[0:1, :], g) * (i < nt - 1).astype(F32)
    row = lax.broadcasted_iota(jnp.int32, (ts, 1), 0)
    prev = jnp.where(row == 0, prev_row, pltpu.roll(hn, 1, axis=0))
    nxt = jnp.where(row == ts - 1, next_row, pltpu.roll(hn, ts - 1, axis=0))
    xx = 0.5 * (prev + nxt) - hn
    for n, o_ref in enumerate(o_refs):
        o_ref[...] = (hn + xx * mix_ref[n:n + 1, :]).astype(BF16)


def _rwkv_prep(h, g, mix):
    bsz, seq, dim = h.shape
    nmix = mix.shape[0]
    ts = min(seq, 512)
    nt = seq // ts
    hb = ts // POOL_HALO
    nhb = seq // POOL_HALO
    spec = pl.BlockSpec((None, ts, dim), lambda b, i: (b, i, 0))
    return pl.pallas_call(
        functools.partial(_prep_kernel, ts=ts),
        grid=(bsz, nt),
        in_specs=[
            spec,
            pl.BlockSpec((None, POOL_HALO, dim), lambda b, i: (b, jnp.maximum(i * hb - 1, 0), 0)),
            pl.BlockSpec((None, POOL_HALO, dim), lambda b, i: (b, jnp.minimum((i + 1) * hb, nhb - 1), 0)),
            pl.BlockSpec((1, dim), lambda b, i: (0, 0)),
            pl.BlockSpec((nmix, dim), lambda b, i: (0, 0)),
        ],
        out_specs=[spec] * nmix,
        out_shape=[jax.ShapeDtypeStruct(h.shape, BF16)] * nmix,
        compiler_params=_cparams(("parallel", "arbitrary")),
        name="rwkv_prep",
    )(h, h, h, g.reshape(1, dim), mix)


def _mm_kernel(x_ref, w_ref, o_ref):
    o_ref[...] = _dot(x_ref[...], w_ref[...])


def _mm_key_kernel(x_ref, w_ref, kk_scale_ref, k_ref, kk_ref):
    k = _dot(x_ref[...], w_ref[...])
    k_ref[...] = k
    ones_bd = _head_ones()
    for c in range(k.shape[1] // LANES):
        cs = slice(c * LANES, (c + 1) * LANES)
        kq = k[:, cs] * kk_scale_ref[:, cs]
        ss = _head_sum(kq * kq, ones_bd)
        kk_ref[:, cs] = kq * lax.rsqrt(jnp.maximum(ss, 1e-24))


def _proj(x2d, w, kk_scale=None):
    t, kdim = x2d.shape
    n = w.shape[1]
    tm = min(t, 1024)
    tn = min(n, 1024)
    x_spec = pl.BlockSpec((tm, kdim), lambda i, j: (i, 0))
    w_spec = pl.BlockSpec((kdim, tn), lambda i, j: (0, j))
    o_spec = pl.BlockSpec((tm, tn), lambda i, j: (i, j))
    o_shape = jax.ShapeDtypeStruct((t, n), F32)
    if kk_scale is None:
        return pl.pallas_call(
            _mm_kernel, grid=(t // tm, n // tn), in_specs=[x_spec, w_spec], out_specs=o_spec,
            out_shape=o_shape, compiler_params=_cparams(("parallel", "arbitrary")), name="rwkv_proj",
        )(x2d, w.astype(BF16))
    return pl.pallas_call(
        _mm_key_kernel, grid=(t // tm, n // tn),
        in_specs=[x_spec, w_spec, pl.BlockSpec((1, tn), lambda i, j: (0, j))],
        out_specs=[o_spec, o_spec], out_shape=[o_shape, o_shape],
        compiler_params=_cparams(("parallel", "arbitrary")), name="rwkv_proj_key",
    )(x2d, w.astype(BF16), kk_scale.reshape(1, n))


def _lora_kernel(x_ref, w1_ref, w2_ref, b_ref, o_ref, *, mode, hid):
    h = _dot(x_ref[...], w1_ref[...])
    if mode == "decay":
        h = jnp.tanh(h)
    elif mode == "gate":
        h = jax.nn.sigmoid(h)
    h = h.astype(BF16)
    for e in range(o_ref.shape[0]):
        z = _dot(h[:, e * hid:(e + 1) * hid], w2_ref[e])
        if mode == "decay":
            z = -jax.nn.sigmoid(b_ref[e] + z) * jnp.exp(F32(-0.5))
        elif mode == "iclr":
            z = jax.nn.sigmoid(b_ref[e] + z)
        o_ref[e] = z


def _lora(x2d, w1, w2, bias, mode):
    t, dim = x2d.shape
    ne, _, rank = w1.shape
    hid = -(-rank // LANES) * LANES
    w1p = jnp.pad(w1, ((0, 0), (0, 0), (0, hid - rank)))
    w1p = jnp.transpose(w1p, (1, 0, 2)).reshape(dim, ne * hid).astype(BF16)
    w2p = jnp.pad(w2, ((0, 0), (0, hid - rank), (0, 0))).astype(BF16)
    tm = min(t, 512)
    return pl.pallas_call(
        functools.partial(_lora_kernel, mode=mode, hid=hid),
        grid=(t // tm,),
        in_specs=[
            pl.BlockSpec((tm, dim), lambda i: (i, 0)),
            pl.BlockSpec((dim, ne * hid), lambda i: (0, 0)),
            pl.BlockSpec((ne, hid, dim), lambda i: (0, 0, 0)),
            pl.BlockSpec((ne, 1, dim), lambda i: (0, 0, 0)),
        ],
        out_specs=pl.BlockSpec((ne, tm, dim), lambda i: (0, i, 0)),
        out_shape=jax.ShapeDtypeStruct((ne, t, dim), F32),
        compiler_params=_cparams(("parallel",)),
        name="rwkv_lora_" + mode,
    )(x2d, w1p, w2p, bias.reshape(ne, 1, dim))


def _pair_blockdiag(x):
    lane = lax.broadcasted_iota(jnp.int32, x.shape, 1)
    top = jnp.where(lane < HEAD_SIZE, x, 0.0)
    bot = jnp.where(lane >= HEAD_SIZE, x, 0.0)
    return jnp.concatenate([top, bot], axis=0).astype(BF16)


def _pair_dot(a, b):
    return _dot(a.astype(BF16), _pair_blockdiag(b))


def _wkv_kernel(r_ref, k_ref, v_ref, kk_ref, lw_ref, a_ref, ka_ref, *rest):
    n_cast = (len(rest) - 2) // 2
    y_ref, s_ref = rest[n_cast], rest[-1]
    for w_ref, wo_ref in zip(rest[:n_cast], rest[n_cast + 1:-1]):
        wo_ref[...] = w_ref[...].astype(BF16)
    d = pl.program_id(0)
    sgn = 1 - 2 * d
    fwd_f = (1 - d).astype(F32)
    L = CHUNK

    @pl.when(pl.program_id(2) == 0)
    def _():
        s_ref[...] = jnp.zeros_like(s_ref)

    t_sq = lax.broadcasted_iota(jnp.int32, (L, L), 0)
    s_sq = lax.broadcasted_iota(jnp.int32, (L, L), 1)
    tri = (((t_sq - s_sq) * sgn) >= 0).astype(BF16)
    t_p = lax.broadcasted_iota(jnp.int32, (L, LANES), 0)
    s_p = lax.broadcasted_iota(jnp.int32, (L, LANES), 1) % HEAD_SIZE
    rel = (t_p - s_p) * sgn
    m_strict = rel > 0
    m_read = (rel + (1 - d)) > 0
    eye = (t_p == s_p).astype(F32)
    row_h = lax.broadcasted_iota(jnp.int32, (LANES, LANES), 0) // HEAD_SIZE
    col_h = lax.broadcasted_iota(jnp.int32, (LANES, LANES), 1) // HEAD_SIZE
    same_head = row_h == col_h
    pairs = range(s_ref.shape[0])
    cols = [slice(p * LANES, (p + 1) * LANES) for p in pairs]

    def each(fn, *lists):
        return [fn(*args) for args in zip(*lists)]

    def cumulative(lw):
        hi, mid, lo = _split3(lw)
        return _dot(tri, hi) + _dot(tri, mid) + _dot(tri, lo)

    lw = [lw_ref[:, cs] for cs in cols]
    c_incl = each(cumulative, lw)
    c_excl = each(lambda c, w: c - w, c_incl, lw)
    tot = each(lambda w: jnp.sum(w, axis=0, keepdims=True), lw)
    kk = [kk_ref[:, cs] for cs in cols]
    a = [a_ref[:, cs] for cs in cols]
    beta = each(lambda x, y: x * y, kk, a)
    kt = [k_ref[:, cs] * (1.0 + (a_p - 1.0) * ka_ref[:, cs]) for cs, a_p in zip(cols, a)]
    ah = each(lambda x, c: -x * jnp.exp(c), kk, c_excl)
    rh = [r_ref[:, cs] * jnp.exp(c + fwd_f * w) for cs, c, w in zip(cols, c_excl, lw)]
    e_neg = each(lambda c: jnp.exp(-c), c_incl)
    lhs = each(lambda x, y: jnp.concatenate([x, y], axis=0).astype(BF16), ah, rh)
    a_b = each(lambda l, b, e: _dot_nt(l, _pair_blockdiag(b * e)), lhs, beta, e_neg)
    a_k = each(lambda l, b, e: _dot_nt(l, _pair_blockdiag(b * e)), lhs, kt, e_neg)
    n_ab = each(lambda m: jnp.where(m_strict, m[:L], 0.0), a_b)
    n_ak = each(lambda m: jnp.where(m_strict, m[:L], 0.0), a_k)
    n_rb = each(lambda m: jnp.where(m_read, m[L:], 0.0), a_b)
    n_rk = each(lambda m: jnp.where(m_read, m[L:], 0.0), a_k)
    v = [v_ref[:, cs] for cs in cols]
    x_ak = each(_pair_dot, n_ak, v)
    y_rk = each(_pair_dot, n_rk, v)
    tm = each(lambda n: eye + n, n_ab)
    pw = n_ab
    span = 1
    while span * 2 < L:
        pw = each(_pair_dot, pw, pw)
        tm = each(lambda t, q: t + _pair_dot(q, t), tm, pw)
        span *= 2
    at = each(_pair_dot, tm, ah)
    ui = each(_pair_dot, tm, x_ak)
    rt = each(lambda x, n, y: x + _pair_dot(n, y), rh, n_rb, at)
    yi = each(lambda n, u, y: _pair_dot(n, u) + y, n_rb, ui, y_rk)
    e_bar = each(lambda t, c: jnp.exp(t - c), tot, c_incl)
    rhs = each(lambda b, k_, e: jnp.concatenate([b * e, k_ * e], axis=0).astype(BF16), beta, kt, e_bar)
    s = [s_ref[p] for p in pairs]
    uy = each(lambda x, y, s_p: _dot_nt(jnp.concatenate([x, y], axis=0).astype(BF16), s_p.astype(BF16)),
              at, rt, s)
    for cs, uy_p, yi_p in zip(cols, uy, yi):
        y_ref[:, cs] = uy_p[L:] + yi_p
    uv_t = each(lambda uy_p, ui_p, v_p: jnp.concatenate([uy_p[:L] + ui_p, v_p], axis=0).T.astype(BF16),
                uy, ui, v)
    upd = each(_dot, uv_t, rhs)
    for p in pairs:
        s_ref[p] = s[p] * jnp.exp(tot[p]) + jnp.where(same_head, upd[p], 0.0)


BF16_SUBLANES = 16


def _wkv(r, k, v, kk, lw, a, k_a, cast_weights=()):
    bsz, seq, dim = r.shape
    nc = seq // CHUNK
    steps = 2 * bsz * nc

    def cidx(d, c):
        return c + d * (nc - 1 - 2 * c)

    def step(d, b, c):
        return (d * bsz + b) * nc + c

    flat = [w.reshape(-1, w.shape[-1]) for w in cast_weights]
    slab = [w.shape[0] // steps for w in flat]
    in_kernel = all(w.shape[0] % steps == 0 and s % BF16_SUBLANES == 0 for w, s in zip(flat, slab))
    if not in_kernel:
        flat, slab = [], []
    w_specs = [pl.BlockSpec((s, w.shape[1]), lambda d, b, c: (step(d, b, c), 0)) for w, s in zip(flat, slab)]
    tok = pl.BlockSpec((None, CHUNK, dim), lambda d, b, c: (b, cidx(d, c), 0))
    dirtok = pl.BlockSpec((None, None, CHUNK, dim), lambda d, b, c: (d, b, cidx(d, c), 0))
    outs = pl.pallas_call(
        _wkv_kernel,
        grid=(2, bsz, nc),
        in_specs=[tok, tok, tok, tok, dirtok, dirtok, pl.BlockSpec((1, dim), lambda d, b, c: (0, 0))] + w_specs,
        out_specs=[dirtok] + w_specs,
        out_shape=[jax.ShapeDtypeStruct((2, bsz, seq, dim), F32)]
        + [jax.ShapeDtypeStruct(w.shape, BF16) for w in flat],
        scratch_shapes=[pltpu.VMEM((dim // LANES, LANES, LANES), F32)],
        compiler_params=_cparams(("arbitrary", "arbitrary", "arbitrary")),
        name="wkv_scan",
    )(r, k, v, kk, lw, a, k_a.reshape(1, dim), *flat)
    if in_kernel:
        return outs[0], [o.reshape(w.shape) for o, w in zip(outs[1:], cast_weights)]
    return outs[0], [w.astype(BF16) for w in cast_weights]


def _rwkv_out_kernel(yf_ref, yb_ref, r_ref, k_ref, v_ref, a_ref, g_ref, h_ref,
                     ka_ref, rk_ref, lw_ref, lb_ref, wo_ref, o_ref, z_ref):
    ones_bd = _head_ones()
    inv_n = 1.0 / HEAD_SIZE
    for c in range(h_ref.shape[1] // LANES):
        cs = slice(c * LANES, (c + 1) * LANES)
        y = yf_ref[:, cs] + yb_ref[:, cs]
        mu = _head_sum(y, ones_bd) * inv_n
        yc = y - mu
        var = _head_sum(yc * yc, ones_bd) * inv_n
        yn = yc * lax.rsqrt(var + LNX_EPS) * lw_ref[:, cs] + lb_ref[:, cs]
        kd = k_ref[:, cs] * (1.0 + (a_ref[:, cs] - 1.0) * ka_ref[:, cs])
        bonus = _head_sum(r_ref[:, cs] * kd * rk_ref[:, cs], ones_bd) * v_ref[:, cs]
        z_ref[:, cs] = ((yn + bonus) * g_ref[:, cs]).astype(BF16)
    o_ref[...] = h_ref[...] + _dot(z_ref[...], wo_ref[...])


def _rwkv_out(y, r, k, v, a, g, h2d, k_a, r_k, lnx_w, lnx_b, w_o):
    t, dim = h2d.shape
    tm = min(t, 256)
    tok = pl.BlockSpec((tm, dim), lambda i: (i, 0))
    vec = pl.BlockSpec((1, dim), lambda i: (0, 0))
    return pl.pallas_call(
        _rwkv_out_kernel,
        grid=(t // tm,),
        in_specs=[pl.BlockSpec((None, tm, dim), lambda i: (0, i, 0)),
                  pl.BlockSpec((None, tm, dim), lambda i: (1, i, 0)),
                  tok, tok, tok,
                  pl.BlockSpec((None, tm, dim), lambda i: (0, i, 0)),
                  tok, tok, vec, vec, vec, vec,
                  pl.BlockSpec((dim, dim), lambda i: (0, 0))],
        out_specs=tok,
        out_shape=jax.ShapeDtypeStruct((t, dim), F32),
        scratch_shapes=[pltpu.VMEM((tm, dim), BF16)],
        compiler_params=_cparams(("parallel",)),
        name="rwkv_out",
    )(y, y, r, k, v, a, g, h2d, k_a.reshape(1, dim), r_k.reshape(1, dim),
      lnx_w.reshape(1, dim), lnx_b.reshape(1, dim), w_o.astype(BF16))


ROUTE_E1, ROUTE_E2, ROUTE_G1, ROUTE_G2, ROUTE_R1, ROUTE_R2 = range(6)


def _router_kernel(x_ref, g_ref, w_ref, hn_ref, route_ref, cnt_ref, *, n_exp, tm):
    i = pl.program_id(0)

    @pl.when(i == 0)
    def _():
        cnt_ref[...] = jnp.zeros_like(cnt_ref)

    hn = _rms(x_ref[...], g_ref[...])
    for c in range(hn_ref.shape[1]):
        hn_ref[:, c, :] = hn[:, c * LANES:(c + 1) * LANES]
    x_hi, x_mid, x_lo = _split3(hn)
    w_hi, w_mid, w_lo = w_ref[0], w_ref[1], w_ref[2]
    logits = (_dot(x_hi, w_hi) + _dot(x_hi, w_mid) + _dot(x_mid, w_hi)
              + _dot(x_hi, w_lo) + _dot(x_mid, w_mid) + _dot(x_lo, w_hi))
    lane = lax.broadcasted_iota(jnp.int32, (tm, LANES), 1)
    neg = F32(-jnp.inf)
    lg = jnp.where(lane < n_exp, logits, neg)
    m1 = jnp.max(lg, axis=-1, keepdims=True)
    i1 = jnp.min(jnp.where(lg == m1, lane, LANES), axis=-1, keepdims=True)
    lg2 = jnp.where(lane == i1, neg, lg)
    m2 = jnp.max(lg2, axis=-1, keepdims=True)
    i2 = jnp.min(jnp.where(lg2 == m2, lane, LANES), axis=-1, keepdims=True)
    e2 = jnp.exp(m2 - m1)
    den = 1.0 + e2
    g1 = 1.0 / den
    g2 = e2 / den
    oh1 = lane == i1
    oh2 = lane == i2
    oh = oh1.astype(F32) + oh2.astype(F32)
    rr = lax.broadcasted_iota(jnp.int32, (tm, tm), 0)
    cc = lax.broadcasted_iota(jnp.int32, (tm, tm), 1)
    before = (cc < rr).astype(BF16)
    prefix = _dot(before, oh.astype(BF16)) + cnt_ref[...]
    r1 = jnp.sum(jnp.where(oh1, prefix, 0.0), axis=-1, keepdims=True)
    r2 = jnp.sum(jnp.where(oh2, prefix, 0.0), axis=-1, keepdims=True)
    cnt_ref[...] += jnp.sum(oh, axis=0, keepdims=True)
    route = jnp.zeros((tm, LANES), F32)
    for col, val in ((ROUTE_E1, i1.astype(F32)), (ROUTE_E2, i2.astype(F32)), (ROUTE_G1, g1),
                     (ROUTE_G2, g2), (ROUTE_R1, r1), (ROUTE_R2, r2)):
        route = jnp.where(lane == col, val, route)
    route_ref[...] = route


def _router(h2d, g, router_w):
    t, dim = h2d.shape
    n_exp = router_w.shape[1]
    tm = min(t, 512)
    wp = jnp.pad(router_w, ((0, 0), (0, LANES - n_exp)))
    w3 = jnp.stack(_split3(wp))
    return pl.pallas_call(
        functools.partial(_router_kernel, n_exp=n_exp, tm=tm),
        grid=(t // tm,),
        in_specs=[pl.BlockSpec((tm, dim), lambda i: (i, 0)),
                  pl.BlockSpec((1, dim), lambda i: (0, 0)),
                  pl.BlockSpec((3, dim, LANES), lambda i: (0, 0, 0))],
        out_specs=[pl.BlockSpec((tm, dim // LANES, LANES), lambda i: (i, 0, 0)),
                   pl.BlockSpec((tm, LANES), lambda i: (i, 0)),
                   pl.BlockSpec((1, LANES), lambda i: (0, 0))],
        out_shape=[jax.ShapeDtypeStruct((t, dim // LANES, LANES), F32),
                   jax.ShapeDtypeStruct((t, LANES), F32),
                   jax.ShapeDtypeStruct((1, LANES), F32)],
        compiler_params=_cparams(("arbitrary",)),
        name="moe_router",
    )(h2d, g.reshape(1, dim), w3)


def _row_gather(src_hbm, dst_vmem, sem, index_of, rows):
    def copy(t):
        return pltpu.make_async_copy(src_hbm.at[pl.ds(index_of(t), 1)], dst_vmem.at[pl.ds(t, 1)], sem)

    def start():
        lax.fori_loop(0, rows, lambda t, c: (copy(t).start(), c)[1], 0, unroll=8)

    def wait():
        lax.fori_loop(0, rows, lambda t, c: (copy(t).wait(), c)[1], 0, unroll=8)

    return start, wait


def _dispatch_kernel(src_ref, hn_ref, xs_ref, buf_ref, sem, *, rows):
    i = pl.program_id(0)
    n = pl.num_programs(0)

    def gather(step):
        slot = step % 2
        return _row_gather(hn_ref, buf_ref.at[slot], sem.at[slot], lambda t: src_ref[step * rows + t], rows)

    @pl.when(i == 0)
    def _():
        gather(i)[0]()

    @pl.when(i + 1 < n)
    def _():
        gather(i + 1)[0]()

    gather(i)[1]()
    slot = i % 2
    for c in range(buf_ref.shape[2]):
        xs_ref[:, c * LANES:(c + 1) * LANES] = buf_ref[slot, :, c, :].astype(BF16)


def _dispatch(hn, src_rows):
    tile_shape = hn.shape[1:]
    dim = tile_shape[0] * tile_shape[1]
    n_rows = src_rows.shape[0]
    rows = min(n_rows, 512)
    return pl.pallas_call(
        functools.partial(_dispatch_kernel, rows=rows),
        grid_spec=pltpu.PrefetchScalarGridSpec(
            num_scalar_prefetch=1,
            grid=(n_rows // rows,),
            in_specs=[pl.BlockSpec(memory_space=pl.ANY)],
            out_specs=pl.BlockSpec((rows, dim), lambda i, s: (i, 0)),
            scratch_shapes=[pltpu.VMEM((2, rows) + tile_shape, F32), pltpu.SemaphoreType.DMA((2,))],
        ),
        out_shape=jax.ShapeDtypeStruct((n_rows, dim), BF16),
        compiler_params=_cparams(("arbitrary",)),
        name="moe_dispatch",
    )(src_rows, hn)


def _expert_kernel(te_ref, nu_ref, x_ref, wg_ref, wu_ref, wd_ref, o_ref, acc_ref):
    del te_ref
    i = pl.program_id(0)
    j = pl.program_id(1)
    used = i < nu_ref[0]

    @pl.when(j == 0)
    def _():
        acc_ref[...] = jnp.zeros_like(acc_ref)

    @pl.when(used)
    def _():
        x = x_ref[...]
        gate = _dot(x, wg_ref[...])
        up = _dot(x, wu_ref[...])
        act = (gate * jax.nn.sigmoid(gate) * up).astype(BF16)
        acc_ref[...] += _dot(act, wd_ref[...])

    @pl.when(j == pl.num_programs(1) - 1)
    def _():
        for c in range(o_ref.shape[1]):
            o_ref[:, c, :] = acc_ref[:, c * LANES:(c + 1) * LANES]


def _expert_tiles(ff):
    for tf in (1024, 512, 256, 128):
        if ff % tf == 0:
            return tf
    return ff


def _experts(xs, tile_expert, n_used, w_gate, w_up, w_down, tm):
    n_rows, dim = xs.shape
    ff = w_gate.shape[2]
    tf = _expert_tiles(ff)
    nj = ff // tf

    def jj(i, j, nu):
        return jnp.where(i < nu[0], j, nj - 1)

    return pl.pallas_call(
        _expert_kernel,
        grid_spec=pltpu.PrefetchScalarGridSpec(
            num_scalar_prefetch=2,
            grid=(n_rows // tm, nj),
            in_specs=[pl.BlockSpec((tm, dim), lambda i, j, te, nu: (i, 0)),
                      pl.BlockSpec((None, dim, tf), lambda i, j, te, nu: (te[i], 0, jj(i, j, nu))),
                      pl.BlockSpec((None, dim, tf), lambda i, j, te, nu: (te[i], 0, jj(i, j, nu))),
                      pl.BlockSpec((None, tf, dim), lambda i, j, te, nu: (te[i], jj(i, j, nu), 0))],
            out_specs=pl.BlockSpec((tm, dim // LANES, LANES), lambda i, j, te, nu: (i, 0, 0)),
            scratch_shapes=[pltpu.VMEM((tm, dim), F32)],
        ),
        out_shape=jax.ShapeDtypeStruct((n_rows, dim // LANES, LANES), F32),
        compiler_params=_cparams(("arbitrary", "arbitrary")),
        name="moe_experts",
    )(tile_expert, n_used, xs, w_gate.astype(BF16), w_up.astype(BF16), w_down.astype(BF16))


def _combine_kernel(slots_ref, h_ref, route_ref, g_ref, y_ref, o_ref, buf_ref, sem, *, rows):
    i = pl.program_id(0)
    n = pl.num_programs(0)

    def gather(step, k):
        slot = step % 2
        return _row_gather(y_ref, buf_ref.at[slot, k], sem.at[slot],
                           lambda t: slots_ref[TOP_K * (step * rows + t) + k], rows)

    @pl.when(i == 0)
    def _():
        for k in range(TOP_K):
            gather(i, k)[0]()

    @pl.when(i + 1 < n)
    def _():
        for k in range(TOP_K):
            gather(i + 1, k)[0]()

    for k in range(TOP_K):
        gather(i, k)[1]()
    slot = i % 2
    route = route_ref[...]
    g1 = route[:, ROUTE_G1:ROUTE_G1 + 1]
    g2 = route[:, ROUTE_G2:ROUTE_G2 + 1]
    dim = h_ref.shape[1]
    ss = jnp.zeros((rows, 1), F32)
    for c in range(dim // LANES):
        cs = slice(c * LANES, (c + 1) * LANES)
        hc = h_ref[:, cs] + g1 * buf_ref[slot, 0, :, c, :] + g2 * buf_ref[slot, 1, :, c, :]
        o_ref[:, cs] = hc
        ss = ss + jnp.sum(hc * hc, axis=-1, keepdims=True)
    o_ref[...] = o_ref[...] * lax.rsqrt(ss * (1.0 / dim) + RMS_EPS) * g_ref[...]


def _combine(h2d, route, slots, y, g):
    t, dim = h2d.shape
    tile_shape = y.shape[1:]
    rows = min(t, 256)
    return pl.pallas_call(
        functools.partial(_combine_kernel, rows=rows),
        grid_spec=pltpu.PrefetchScalarGridSpec(
            num_scalar_prefetch=1,
            grid=(t // rows,),
            in_specs=[pl.BlockSpec((rows, dim), lambda i, s: (i, 0)),
                      pl.BlockSpec((rows, LANES), lambda i, s: (i, 0)),
                      pl.BlockSpec((1, dim), lambda i, s: (0, 0)),
                      pl.BlockSpec(memory_space=pl.ANY)],
            out_specs=pl.BlockSpec((rows, dim), lambda i, s: (i, 0)),
            scratch_shapes=[pltpu.VMEM((2, TOP_K, rows) + tile_shape, F32), pltpu.SemaphoreType.DMA((2,))],
        ),
        out_shape=jax.ShapeDtypeStruct((t, dim), F32),
        compiler_params=_cparams(("arbitrary",)),
        name="moe_combine",
    )(slots, h2d, route, g.reshape(1, dim), y)


def _moe_layer(h2d, norm_g, router_w, w_gate, w_up, w_down, final_g):
    t, dim = h2d.shape
    n_exp = router_w.shape[1]
    tm = min(t, 512)
    hn, route, counts = _router(h2d, norm_g, router_w)
    counts = counts[0, :n_exp].astype(jnp.int32)
    tiles = (counts + tm - 1) // tm
    tile_end = jnp.cumsum(tiles)
    row_start = (tile_end - tiles) * tm
    e12 = route[:, ROUTE_E1:ROUTE_E2 + 1].astype(jnp.int32)
    r12 = route[:, ROUTE_R1:ROUTE_R2 + 1].astype(jnp.int32)
    slots = (row_start[e12] + r12).reshape(-1)
    n_tiles = TOP_K * t // tm + n_exp
    n_used = tile_end[-1:]
    tile_ids = jnp.minimum(jnp.arange(n_tiles, dtype=jnp.int32), n_used[0] - 1)
    tile_expert = jnp.sum(tile_ids[:, None] >= tile_end[None, :], axis=1).astype(jnp.int32)
    token = jnp.repeat(jnp.arange(t, dtype=jnp.int32), TOP_K)
    src_rows = jnp.zeros((n_tiles * tm,), jnp.int32).at[slots].set(token)
    xs = _dispatch(hn, src_rows)
    y = _experts(xs, tile_expert, n_used, w_gate, w_up, w_down, tm)
    return _combine(h2d, route, slots, y, final_g)


def kernel(x, l0_norm_mix, l0_pool_w, l0_pool_scale, l0_norm_ffn, l0_ffn_w_gate, l0_ffn_w_up, l0_ffn_w_down, l1_norm_mix, l1_mix, l1_w_r, l1_w_k, l1_w_v, l1_w_o, l1_decay_w0, l1_decay_w1, l1_decay_w2, l1_iclr_a0, l1_iclr_a1, l1_iclr_a2, l1_gate_g1, l1_gate_g2, l1_k_k, l1_k_a, l1_r_k, l1_lnx_w, l1_lnx_b, l1_norm_ffn, l1_router, l1_moe_w_gate, l1_moe_w_up, l1_moe_w_down, final_norm):
    bsz, seq, dim = x.shape
    t = bsz * seq
    h = _pool_layer(x, l0_norm_mix, l0_pool_w, l0_pool_scale)
    h = _ffn_layer(h.reshape(t, dim), l0_norm_ffn, l0_ffn_w_gate, l0_ffn_w_up, l0_ffn_w_down)
    xr, xw, xk, xv, xa, xg = [m.reshape(t, dim) for m in _rwkv_prep(h.reshape(bsz, seq, dim), l1_norm_mix, l1_mix)]
    r = _proj(xr, l1_w_r)
    k, kk = _proj(xk, l1_w_k, kk_scale=l1_k_k)
    v = _proj(xv, l1_w_v)
    lw = _lora(xw, l1_decay_w1, l1_decay_w2, l1_decay_w0, "decay")
    a = _lora(xa, l1_iclr_a1, l1_iclr_a2, l1_iclr_a0, "iclr")
    g = _lora(xg, l1_gate_g1[None], l1_gate_g2[None], jnp.zeros((1, dim), F32), "gate")[0]
    b3 = lambda z: z.reshape(bsz, seq, dim)
    b4 = lambda z: z.reshape(2, bsz, seq, dim)
    y, (moe_wg, moe_wu, moe_wd) = _wkv(b3(r), b3(k), b3(v), b3(kk), b4(lw), b4(a), l1_k_a,
                                       cast_weights=(l1_moe_w_gate, l1_moe_w_up, l1_moe_w_down))
    h = _rwkv_out(y.reshape(2, t, dim), r, k, v, a, g, h, l1_k_a, l1_r_k, l1_lnx_w, l1_lnx_b, l1_w_o)
    out = _moe_layer(h, l1_norm_ffn, l1_router, moe_wg, moe_wu, moe_wd, final_norm)
    return out.reshape(bsz, seq, dim)
```

```python
import functools

import jax
import jax.numpy as jnp
from jax import lax
from jax.experimental import pallas as pl
from jax.experimental.pallas import tpu as pltpu

F32 = jnp.float32
BF16 = jnp.bfloat16

HEAD_SIZE = 64
LANES = 128
POOL_WINDOWS = (2, 4, 8, 16)
POOL_HALO = 8
TOP_K = 2
RMS_EPS = 1e-6
LNX_EPS = 64e-5
CHUNK = 64
VMEM_LIMIT_BYTES = 56 * 1024 * 1024


def _cparams(sem):
    return pltpu.CompilerParams(dimension_semantics=sem, vmem_limit_bytes=VMEM_LIMIT_BYTES)


def _dot(a, b):
    return jnp.dot(a, b, preferred_element_type=F32)


def _dot_nt(a, b):
    return lax.dot_general(a, b, (((1,), (1,)), ((), ())), preferred_element_type=F32)


def _rms(x, g):
    return x * lax.rsqrt(jnp.mean(x * x, axis=-1, keepdims=True) + RMS_EPS) * g


def _split3(x):
    hi = x.astype(BF16)
    r1 = x - hi.astype(F32)
    mid = r1.astype(BF16)
    lo = (r1 - mid.astype(F32)).astype(BF16)
    return hi, mid, lo


def _head_ones():
    r = lax.broadcasted_iota(jnp.int32, (LANES, LANES), 0) // HEAD_SIZE
    c = lax.broadcasted_iota(jnp.int32, (LANES, LANES), 1) // HEAD_SIZE
    return (r == c).astype(BF16)


def _head_sum(x, ones_bd):
    hi, mid, _ = _split3(x)
    return _dot(hi, ones_bd) + _dot(mid, ones_bd)


def _pool_kernel(x_ref, xp_ref, xn_ref, g_ref, pw_ref, ps_ref, o_ref, ext_ref, *, seq, ts, cg):
    i = pl.program_id(1)
    nt = pl.num_programs(1)
    g = g_ref[...]
    x = x_ref[...]
    hn = _rms(x, g)
    has_prev = (i > 0).astype(F32)
    has_next = (i < nt - 1).astype(F32)
    ext_ref[0:POOL_HALO, :] = _rms(xp_ref[...], g) * has_prev
    ext_ref[POOL_HALO:POOL_HALO + ts, :] = hn
    ext_ref[POOL_HALO + ts:POOL_HALO + ts + POOL_HALO, :] = _rms(xn_ref[...], g) * has_next
    pos = i * ts + lax.broadcasted_iota(jnp.int32, (ts, 1), 0)
    for gi, w in enumerate(POOL_WINDOWS):
        cs = slice(gi * cg, (gi + 1) * cg)
        lo_off = -(w // 2)
        hi_off = w - w // 2 - 1
        tot = None
        for o in range(lo_off, hi_off + 1):
            part = ext_ref[POOL_HALO + o:POOL_HALO + o + ts, cs]
            tot = part if tot is None else tot + part
        lo = jnp.maximum(pos + lo_off, 0)
        hi = jnp.minimum(pos + hi_off + 1, seq)
        cnt = (hi - lo).astype(F32)
        pooled = tot / cnt - hn[:, cs]
        mixed = _dot(pooled.astype(BF16), pw_ref[gi])
        o_ref[:, cs] = x[:, cs] + mixed * ps_ref[:, cs]


def _pool_layer(x, g, pool_w, pool_scale):
    bsz, seq, dim = x.shape
    ng = len(POOL_WINDOWS)
    cg = dim // ng
    ts = min(seq, 512)
    nt = seq // ts
    hb = ts // POOL_HALO
    nhb = seq // POOL_HALO
    kern = functools.partial(_pool_kernel, seq=seq, ts=ts, cg=cg)
    return pl.pallas_call(
        kern,
        grid=(bsz, nt),
        in_specs=[
            pl.BlockSpec((None, ts, dim), lambda b, i: (b, i, 0)),
            pl.BlockSpec((None, POOL_HALO, dim), lambda b, i: (b, jnp.maximum(i * hb - 1, 0), 0)),
            pl.BlockSpec((None, POOL_HALO, dim), lambda b, i: (b, jnp.minimum((i + 1) * hb, nhb - 1), 0)),
            pl.BlockSpec((1, dim), lambda b, i: (0, 0)),
            pl.BlockSpec((ng, cg, cg), lambda b, i: (0, 0, 0)),
            pl.BlockSpec((1, dim), lambda b, i: (0, 0)),
        ],
        out_specs=pl.BlockSpec((None, ts, dim), lambda b, i: (b, i, 0)),
        out_shape=jax.ShapeDtypeStruct(x.shape, F32),
        scratch_shapes=[pltpu.VMEM((ts + 2 * POOL_HALO, dim), F32)],
        compiler_params=_cparams(("parallel", "arbitrary")),
        name="pool_layer",
    )(x, x, x, g.reshape(1, dim), pool_w.astype(BF16), pool_scale.reshape(1, dim))


def _ffn_kernel(x_ref, g_ref, wg_ref, wu_ref, wd_ref, o_ref, hn_ref):
    j = pl.program_id(1)

    @pl.when(j == 0)
    def _():
        x = x_ref[...]
        hn_ref[...] = _rms(x, g_ref[...]).astype(BF16)
        o_ref[...] = x

    hn = hn_ref[...]
    gate = _dot(hn, wg_ref[...])
    up = _dot(hn, wu_ref[...])
    act = (gate * jax.nn.sigmoid(gate) * up).astype(BF16)
    o_ref[...] += _dot(act, wd_ref[...])


def _ffn_layer(x2d, g, w_gate, w_up, w_down):
    t, dim = x2d.shape
    ff = w_gate.shape[1]
    tm = min(t, 512)
    tf = 512 if ff % 512 == 0 else ff
    return pl.pallas_call(
        _ffn_kernel,
        grid=(t // tm, ff // tf),
        in_specs=[
            pl.BlockSpec((tm, dim), lambda i, j: (i, 0)),
            pl.BlockSpec((1, dim), lambda i, j: (0, 0)),
            pl.BlockSpec((dim, tf), lambda i, j: (0, j)),
            pl.BlockSpec((dim, tf), lambda i, j: (0, j)),
            pl.BlockSpec((tf, dim), lambda i, j: (j, 0)),
        ],
        out_specs=pl.BlockSpec((tm, dim), lambda i, j: (i, 0)),
        out_shape=jax.ShapeDtypeStruct((t, dim), F32),
        scratch_shapes=[pltpu.VMEM((tm, dim), BF16)],
        compiler_params=_cparams(("parallel", "arbitrary")),
        name="ffn_dense",
    )(x2d, g.reshape(1, dim), w_gate.astype(BF16), w_up.astype(BF16), w_down.astype(BF16))


def _prep_kernel(x_ref, xp_ref, xn_ref, g_ref, mix_ref, *o_refs, ts):
    i = pl.program_id(1)
    nt = pl.num_programs(1)
    g = g_ref[...]
    hn = _rms(x_ref[...], g)
    prev_row = _rms(xp_ref[POOL_HALO - 1:POOL_HALO, :], g) * (i > 0).astype(F32)
    next_row = _rms(xn_ref[0:1, :], g) * (i < nt - 1).astype(F32)
    row = lax.broadcasted_iota(jnp.int32, (ts, 1), 0)
    prev = jnp.where(row == 0, prev_row, pltpu.roll(hn, 1, axis=0))
    nxt = jnp.where(row == ts - 1, next_row, pltpu.roll(hn, ts - 1, axis=0))
    xx = 0.5 * (prev + nxt) - hn
    for n, o_ref in enumerate(o_refs):
        o_ref[...] = (hn + xx * mix_ref[n:n + 1, :]).astype(BF16)


def _rwkv_prep(h, g, mix):
    bsz, seq, dim = h.shape
    nmix = mix.shape[0]
    ts = min(seq, 512)
    nt = seq // ts
    hb = ts // POOL_HALO
    nhb = seq // POOL_HALO
    spec = pl.BlockSpec((None, ts, dim), lambda b, i: (b, i, 0))
    return pl.pallas_call(
        functools.partial(_prep_kernel, ts=ts),
        grid=(bsz, nt),
        in_specs=[
            spec,
            pl.BlockSpec((None, POOL_HALO, dim), lambda b, i: (b, jnp.maximum(i * hb - 1, 0), 0)),
            pl.BlockSpec((None, POOL_HALO, dim), lambda b, i: (b, jnp.minimum((i + 1) * hb, nhb - 1), 0)),
            pl.BlockSpec((1, dim), lambda b, i: (0, 0)),
            pl.BlockSpec((nmix, dim), lambda b, i: (0, 0)),
        ],
        out_specs=[spec] * nmix,
        out_shape=[jax.ShapeDtypeStruct(h.shape, BF16)] * nmix,
        compiler_params=_cparams(("parallel", "arbitrary")),
        name="rwkv_prep",
    )(h, h, h, g.reshape(1, dim), mix)


def _mm_kernel(x_ref, w_ref, o_ref):
    o_ref[...] = _dot(x_ref[...], w_ref[...])


def _mm_key_kernel(x_ref, w_ref, kk_scale_ref, k_ref, kk_ref):
    k = _dot(x_ref[...], w_ref[...])
    k_ref[...] = k
    ones_bd = _head_ones()
    for c in range(k.shape[1] // LANES):
        cs = slice(c * LANES, (c + 1) * LANES)
        kq = k[:, cs] * kk_scale_ref[:, cs]
        ss = _head_sum(kq * kq, ones_bd)
        kk_ref[:, cs] = kq * lax.rsqrt(jnp.maximum(ss, 1e-24))


def _proj(x2d, w, kk_scale=None):
    t, kdim = x2d.shape
    n = w.shape[1]
    tm = min(t, 1024)
    tn = min(n, 1024)
    x_spec = pl.BlockSpec((tm, kdim), lambda i, j: (i, 0))
    w_spec = pl.BlockSpec((kdim, tn), lambda i, j: (0, j))
    o_spec = pl.BlockSpec((tm, tn), lambda i, j: (i, j))
    o_shape = jax.ShapeDtypeStruct((t, n), F32)
    if kk_scale is None:
        return pl.pallas_call(
            _mm_kernel, grid=(t // tm, n // tn), in_specs=[x_spec, w_spec], out_specs=o_spec,
            out_shape=o_shape, compiler_params=_cparams(("parallel", "arbitrary")), name="rwkv_proj",
        )(x2d, w.astype(BF16))
    return pl.pallas_call(
        _mm_key_kernel, grid=(t // tm, n // tn),
        in_specs=[x_spec, w_spec, pl.BlockSpec((1, tn), lambda i, j: (0, j))],
        out_specs=[o_spec, o_spec], out_shape=[o_shape, o_shape],
        compiler_params=_cparams(("parallel", "arbitrary")), name="rwkv_proj_key",
    )(x2d, w.astype(BF16), kk_scale.reshape(1, n))


def _lora_kernel(x_ref, w1_ref, w2_ref, b_ref, o_ref, *, mode, hid):
    h = _dot(x_ref[...], w1_ref[...])
    if mode == "decay":
        h = jnp.tanh(h)
    elif mode == "gate":
        h = jax.nn.sigmoid(h)
    h = h.astype(BF16)
    for e in range(o_ref.shape[0]):
        z = _dot(h[:, e * hid:(e + 1) * hid], w2_ref[e])
        if mode == "decay":
            z = -jax.nn.sigmoid(b_ref[e] + z) * jnp.exp(F32(-0.5))
        elif mode == "iclr":
            z = jax.nn.sigmoid(b_ref[e] + z)
        o_ref[e] = z


def _lora(x2d, w1, w2, bias, mode):
    t, dim = x2d.shape
    ne, _, rank = w1.shape
    hid = -(-rank // LANES) * LANES
    w1p = jnp.pad(w1, ((0, 0), (0, 0), (0, hid - rank)))
    w1p = jnp.transpose(w1p, (1, 0, 2)).reshape(dim, ne * hid).astype(BF16)
    w2p = jnp.pad(w2, ((0, 0), (0, hid - rank), (0, 0))).astype(BF16)
    tm = min(t, 512)
    return pl.pallas_call(
        functools.partial(_lora_kernel, mode=mode, hid=hid),
        grid=(t // tm,),
        in_specs=[
            pl.BlockSpec((tm, dim), lambda i: (i, 0)),
            pl.BlockSpec((dim, ne * hid), lambda i: (0, 0)),
            pl.BlockSpec((ne, hid, dim), lambda i: (0, 0, 0)),
            pl.BlockSpec((ne, 1, dim), lambda i: (0, 0, 0)),
        ],
        out_specs=pl.BlockSpec((ne, tm, dim), lambda i: (0, i, 0)),
        out_shape=jax.ShapeDtypeStruct((ne, t, dim), F32),
        compiler_params=_cparams(("parallel",)),
        name="rwkv_lora_" + mode,
    )(x2d, w1p, w2p, bias.reshape(ne, 1, dim))


def _pair_blockdiag(x):
    lane = lax.broadcasted_iota(jnp.int32, x.shape, 1)
    top = jnp.where(lane < HEAD_SIZE, x, 0.0)
    bot = jnp.where(lane >= HEAD_SIZE, x, 0.0)
    return jnp.concatenate([top, bot], axis=0).astype(BF16)


def _pair_dot(a, b):
    return _dot(a.astype(BF16), _pair_blockdiag(b))


def _wkv_kernel(r_ref, k_ref, v_ref, kk_ref, lw_ref, a_ref, ka_ref, *rest):
    n_cast = (len(rest) - 2) // 2
    y_ref, s_ref = rest[n_cast], rest[-1]
    for w_ref, wo_ref in zip(rest[:n_cast], rest[n_cast + 1:-1]):
        wo_ref[...] = w_ref[...].astype(BF16)
    d = pl.program_id(0)
    sgn = 1 - 2 * d
    fwd_f = (1 - d).astype(F32)
    L = CHUNK

    @pl.when(pl.program_id(2) == 0)
    def _():
        s_ref[...] = jnp.zeros_like(s_ref)

    t_sq = lax.broadcasted_iota(jnp.int32, (L, L), 0)
    s_sq = lax.broadcasted_iota(jnp.int32, (L, L), 1)
    tri = (((t_sq - s_sq) * sgn) >= 0).astype(BF16)
    t_p = lax.broadcasted_iota(jnp.int32, (L, LANES), 0)
    s_p = lax.broadcasted_iota(jnp.int32, (L, LANES), 1) % HEAD_SIZE
    rel = (t_p - s_p) * sgn
    m_strict = rel > 0
    m_read = (rel + (1 - d)) > 0
    eye = (t_p == s_p).astype(F32)
    row_h = lax.broadcasted_iota(jnp.int32, (LANES, LANES), 0) // HEAD_SIZE
    col_h = lax.broadcasted_iota(jnp.int32, (LANES, LANES), 1) // HEAD_SIZE
    same_head = row_h == col_h
    n_pairs = s_ref.shape[0]
    n_sub = r_ref.shape[0] // L
    sub_rows = [pl.ds(pl.multiple_of((i + d * (n_sub - 1 - 2 * i)) * L, L), L) for i in range(n_sub)]
    cols = [slice(p * LANES, (p + 1) * LANES) for p in range(n_pairs)]
    rows = [rw for rw in sub_rows for _ in cols]
    ccol = [cs for _ in sub_rows for cs in cols]

    def each(fn, *lists):
        return [fn(*args) for args in zip(*lists)]

    def load(ref):
        return [ref[rw, cs] for rw, cs in zip(rows, ccol)]

    def cumulative(lw):
        m = _dot(tri, jnp.concatenate(_split3(lw), axis=1))
        return m[:, :LANES] + m[:, LANES:2 * LANES] + m[:, 2 * LANES:]

    def halves(m):
        return m[:, :LANES], m[:, LANES:]

    def lanes2(x, y):
        return jnp.concatenate([_pair_blockdiag(x), _pair_blockdiag(y)], axis=1)

    lw = load(lw_ref)
    c_incl = each(cumulative, lw)
    c_excl = each(lambda c, w: c - w, c_incl, lw)
    tot = each(lambda w: jnp.sum(w, axis=0, keepdims=True), lw)
    kk = load(kk_ref)
    a = load(a_ref)
    beta = each(lambda x, y: x * y, kk, a)
    kt = each(lambda x, a_p, cs: x * (1.0 + (a_p - 1.0) * ka_ref[:, cs]), load(k_ref), a, ccol)
    ah = each(lambda x, c: -x * jnp.exp(c), kk, c_excl)
    rh = each(lambda x, c, w: x * jnp.exp(c + fwd_f * w), load(r_ref), c_excl, lw)
    e_neg = each(lambda c: jnp.exp(-c), c_incl)
    lhs = each(lambda x, y: jnp.concatenate([x, y], axis=0).astype(BF16), ah, rh)
    a_bk = each(lambda l, b, k_, e: _dot_nt(l, jnp.concatenate([_pair_blockdiag(b * e), _pair_blockdiag(k_ * e)],
                                                               axis=0)), lhs, beta, kt, e_neg)
    n_ab = each(lambda m: jnp.where(m_strict, m[:L, :LANES], 0.0), a_bk)
    n_ak = each(lambda m: jnp.where(m_strict, m[:L, LANES:], 0.0), a_bk)
    n_rb = each(lambda m: jnp.where(m_read, m[L:, :LANES], 0.0), a_bk)
    n_rk = each(lambda m: jnp.where(m_read, m[L:, LANES:], 0.0), a_bk)
    v = load(v_ref)
    xy = each(lambda x, y, v_p: _dot(jnp.concatenate([x, y], axis=0).astype(BF16), _pair_blockdiag(v_p)),
              n_ak, n_rk, v)
    tm = each(lambda n: eye + n, n_ab)
    pw = each(_pair_dot, n_ab, n_ab)
    for _ in range(1, L.bit_length() - 2):
        st = each(lambda q, t: _dot(jnp.concatenate([q, t], axis=0).astype(BF16), _pair_blockdiag(q)), pw, tm)
        tm = each(lambda t, m: t + m[L:], tm, st)
        pw = each(lambda m: m[:L], st)
    tm = each(lambda t, q: t + _pair_dot(t, q), tm, pw)
    at_ui = each(lambda t, x, m: halves(_dot(t.astype(BF16), lanes2(x, m[:L]))), tm, ah, xy)
    rt_yi = each(lambda n, au: halves(_dot(n.astype(BF16), lanes2(au[0], au[1]))), n_rb, at_ui)
    rt = each(lambda x, ry: x + ry[0], rh, rt_yi)
    yi = each(lambda ry, m: ry[1] + m[L:], rt_yi, xy)
    e_bar = each(lambda t, c: jnp.exp(t - c), tot, c_incl)
    rhs = each(lambda b, k_, e: jnp.concatenate([b * e, k_ * e], axis=0).astype(BF16), beta, kt, e_bar)
    for i, rw in enumerate(sub_rows):
        ch = slice(i * n_pairs, (i + 1) * n_pairs)
        s = [s_ref[p] for p in range(n_pairs)]
        uy = each(lambda au, x, s_p: _dot_nt(jnp.concatenate([au[0], x], axis=0).astype(BF16), s_p.astype(BF16)),
                  at_ui[ch], rt[ch], s)
        for cs, uy_p, yi_p in zip(cols, uy, yi[ch]):
            y_ref[rw, cs] = uy_p[L:] + yi_p
        uv_t = each(lambda uy_p, au, v_p: jnp.concatenate([uy_p[:L] + au[1], v_p], axis=0).T.astype(BF16),
                    uy, at_ui[ch], v[ch])
        upd = each(_dot, uv_t, rhs[ch])
        for p in range(n_pairs):
            s_ref[p] = s[p] * jnp.exp(tot[ch][p]) + jnp.where(same_head, upd[p], 0.0)


BF16_SUBLANES = 16
WKV_CHUNKS_PER_STEP = 2


def _wkv(r, k, v, kk, lw, a, k_a, cast_weights=()):
    bsz, seq, dim = r.shape
    sub = WKV_CHUNKS_PER_STEP if (seq // CHUNK) % WKV_CHUNKS_PER_STEP == 0 else 1
    rows = sub * CHUNK
    nc = seq // rows
    steps = 2 * bsz * nc

    def cidx(d, c):
        return c + d * (nc - 1 - 2 * c)

    def step(d, b, c):
        return (d * bsz + b) * nc + c

    flat = [w.reshape(-1, w.shape[-1]) for w in cast_weights]
    slab = [w.shape[0] // steps for w in flat]
    in_kernel = all(w.shape[0] % steps == 0 and s % BF16_SUBLANES == 0 for w, s in zip(flat, slab))
    if not in_kernel:
        flat, slab = [], []
    w_specs = [pl.BlockSpec((s, w.shape[1]), lambda d, b, c: (step(d, b, c), 0)) for w, s in zip(flat, slab)]
    tok = pl.BlockSpec((None, rows, dim), lambda d, b, c: (b, cidx(d, c), 0))
    dirtok = pl.BlockSpec((None, None, rows, dim), lambda d, b, c: (d, b, cidx(d, c), 0))
    outs = pl.pallas_call(
        _wkv_kernel,
        grid=(2, bsz, nc),
        in_specs=[tok, tok, tok, tok, dirtok, dirtok, pl.BlockSpec((1, dim), lambda d, b, c: (0, 0))] + w_specs,
        out_specs=[dirtok] + w_specs,
        out_shape=[jax.ShapeDtypeStruct((2, bsz, seq, dim), F32)]
        + [jax.ShapeDtypeStruct(w.shape, BF16) for w in flat],
        scratch_shapes=[pltpu.VMEM((dim // LANES, LANES, LANES), F32)],
        compiler_params=_cparams(("arbitrary", "arbitrary", "arbitrary")),
        name="wkv_scan",
    )(r, k, v, kk, lw, a, k_a.reshape(1, dim), *flat)
    if in_kernel:
        return outs[0], [o.reshape(w.shape) for o, w in zip(outs[1:], cast_weights)]
    return outs[0], [w.astype(BF16) for w in cast_weights]


def _rwkv_out_kernel(yf_ref, yb_ref, r_ref, k_ref, v_ref, a_ref, g_ref, h_ref,
                     ka_ref, rk_ref, lw_ref, lb_ref, wo_ref, o_ref, z_ref):
    ones_bd = _head_ones()
    inv_n = 1.0 / HEAD_SIZE
    for c in range(h_ref.shape[1] // LANES):
        cs = slice(c * LANES, (c + 1) * LANES)
        y = yf_ref[:, cs] + yb_ref[:, cs]
        mu = _head_sum(y, ones_bd) * inv_n
        yc = y - mu
        var = _head_sum(yc * yc, ones_bd) * inv_n
        yn = yc * lax.rsqrt(var + LNX_EPS) * lw_ref[:, cs] + lb_ref[:, cs]
        kd = k_ref[:, cs] * (1.0 + (a_ref[:, cs] - 1.0) * ka_ref[:, cs])
        bonus = _head_sum(r_ref[:, cs] * kd * rk_ref[:, cs], ones_bd) * v_ref[:, cs]
        z_ref[:, cs] = ((yn + bonus) * g_ref[:, cs]).astype(BF16)
    o_ref[...] = h_ref[...] + _dot(z_ref[...], wo_ref[...])


def _rwkv_out(y, r, k, v, a, g, h2d, k_a, r_k, lnx_w, lnx_b, w_o):
    t, dim = h2d.shape
    tm = min(t, 256)
    tok = pl.BlockSpec((tm, dim), lambda i: (i, 0))
    vec = pl.BlockSpec((1, dim), lambda i: (0, 0))
    return pl.pallas_call(
        _rwkv_out_kernel,
        grid=(t // tm,),
        in_specs=[pl.BlockSpec((None, tm, dim), lambda i: (0, i, 0)),
                  pl.BlockSpec((None, tm, dim), lambda i: (1, i, 0)),
                  tok, tok, tok,
                  pl.BlockSpec((None, tm, dim), lambda i: (0, i, 0)),
                  tok, tok, vec, vec, vec, vec,
                  pl.BlockSpec((dim, dim), lambda i: (0, 0))],
        out_specs=tok,
        out_shape=jax.ShapeDtypeStruct((t, dim), F32),
        scratch_shapes=[pltpu.VMEM((tm, dim), BF16)],
        compiler_params=_cparams(("parallel",)),
        name="rwkv_out",
    )(y, y, r, k, v, a, g, h2d, k_a.reshape(1, dim), r_k.reshape(1, dim),
      lnx_w.reshape(1, dim), lnx_b.reshape(1, dim), w_o.astype(BF16))


ROUTE_E1, ROUTE_E2, ROUTE_G1, ROUTE_G2, ROUTE_R1, ROUTE_R2 = range(6)


def _router_kernel(x_ref, g_ref, w_ref, hn_ref, route_ref, cnt_ref, *, n_exp, tm):
    i = pl.program_id(0)

    @pl.when(i == 0)
    def _():
        cnt_ref[...] = jnp.zeros_like(cnt_ref)

    hn = _rms(x_ref[...], g_ref[...])
    for c in range(hn_ref.shape[1]):
        hn_ref[:, c, :] = hn[:, c * LANES:(c + 1) * LANES]
    x_hi, x_mid, x_lo = _split3(hn)
    w_hi, w_mid, w_lo = w_ref[0], w_ref[1], w_ref[2]
    logits = (_dot(x_hi, w_hi) + _dot(x_hi, w_mid) + _dot(x_mid, w_hi)
              + _dot(x_hi, w_lo) + _dot(x_mid, w_mid) + _dot(x_lo, w_hi))
    lane = lax.broadcasted_iota(jnp.int32, (tm, LANES), 1)
    neg = F32(-jnp.inf)
    lg = jnp.where(lane < n_exp, logits, neg)
    m1 = jnp.max(lg, axis=-1, keepdims=True)
    i1 = jnp.min(jnp.where(lg == m1, lane, LANES), axis=-1, keepdims=True)
    lg2 = jnp.where(lane == i1, neg, lg)
    m2 = jnp.max(lg2, axis=-1, keepdims=True)
    i2 = jnp.min(jnp.where(lg2 == m2, lane, LANES), axis=-1, keepdims=True)
    e2 = jnp.exp(m2 - m1)
    den = 1.0 + e2
    g1 = 1.0 / den
    g2 = e2 / den
    oh1 = lane == i1
    oh2 = lane == i2
    oh = oh1.astype(F32) + oh2.astype(F32)
    rr = lax.broadcasted_iota(jnp.int32, (tm, tm), 0)
    cc = lax.broadcasted_iota(jnp.int32, (tm, tm), 1)
    before = (cc < rr).astype(BF16)
    prefix = _dot(before, oh.astype(BF16)) + cnt_ref[...]
    r1 = jnp.sum(jnp.where(oh1, prefix, 0.0), axis=-1, keepdims=True)
    r2 = jnp.sum(jnp.where(oh2, prefix, 0.0), axis=-1, keepdims=True)
    cnt_ref[...] += jnp.sum(oh, axis=0, keepdims=True)
    route = jnp.zeros((tm, LANES), F32)
    for col, val in ((ROUTE_E1, i1.astype(F32)), (ROUTE_E2, i2.astype(F32)), (ROUTE_G1, g1),
                     (ROUTE_G2, g2), (ROUTE_R1, r1), (ROUTE_R2, r2)):
        route = jnp.where(lane == col, val, route)
    route_ref[...] = route


def _router(h2d, g, router_w):
    t, dim = h2d.shape
    n_exp = router_w.shape[1]
    tm = min(t, 512)
    wp = jnp.pad(router_w, ((0, 0), (0, LANES - n_exp)))
    w3 = jnp.stack(_split3(wp))
    return pl.pallas_call(
        functools.partial(_router_kernel, n_exp=n_exp, tm=tm),
        grid=(t // tm,),
        in_specs=[pl.BlockSpec((tm, dim), lambda i: (i, 0)),
                  pl.BlockSpec((1, dim), lambda i: (0, 0)),
                  pl.BlockSpec((3, dim, LANES), lambda i: (0, 0, 0))],
        out_specs=[pl.BlockSpec((tm, dim // LANES, LANES), lambda i: (i, 0, 0)),
                   pl.BlockSpec((tm, LANES), lambda i: (i, 0)),
                   pl.BlockSpec((1, LANES), lambda i: (0, 0))],
        out_shape=[jax.ShapeDtypeStruct((t, dim // LANES, LANES), F32),
                   jax.ShapeDtypeStruct((t, LANES), F32),
                   jax.ShapeDtypeStruct((1, LANES), F32)],
        compiler_params=_cparams(("arbitrary",)),
        name="moe_router",
    )(h2d, g.reshape(1, dim), w3)


def _row_gather(src_hbm, dst_vmem, sem, index_of, rows):
    def copy(t):
        return pltpu.make_async_copy(src_hbm.at[pl.ds(index_of(t), 1)], dst_vmem.at[pl.ds(t, 1)], sem)

    def start():
        lax.fori_loop(0, rows, lambda t, c: (copy(t).start(), c)[1], 0, unroll=8)

    def wait():
        for t in range(rows):
            pltpu.make_async_copy(src_hbm.at[pl.ds(0, 1)], dst_vmem.at[pl.ds(t, 1)], sem).wait()

    return start, wait


def _dispatch_kernel(src_ref, hn_ref, xs_ref, buf_ref, sem, *, rows):
    i = pl.program_id(0)
    n = pl.num_programs(0)

    def gather(step):
        slot = step % 2
        return _row_gather(hn_ref, buf_ref.at[slot], sem.at[slot], lambda t: src_ref[step * rows + t], rows)

    @pl.when(i == 0)
    def _():
        gather(i)[0]()

    @pl.when(i + 1 < n)
    def _():
        gather(i + 1)[0]()

    gather(i)[1]()
    slot = i % 2
    for c in range(buf_ref.shape[2]):
        xs_ref[:, c * LANES:(c + 1) * LANES] = buf_ref[slot, :, c, :].astype(BF16)


def _dispatch(hn, src_rows):
    tile_shape = hn.shape[1:]
    dim = tile_shape[0] * tile_shape[1]
    n_rows = src_rows.shape[0]
    rows = min(n_rows, 512)
    return pl.pallas_call(
        functools.partial(_dispatch_kernel, rows=rows),
        grid_spec=pltpu.PrefetchScalarGridSpec(
            num_scalar_prefetch=1,
            grid=(n_rows // rows,),
            in_specs=[pl.BlockSpec(memory_space=pl.ANY)],
            out_specs=pl.BlockSpec((rows, dim), lambda i, s: (i, 0)),
            scratch_shapes=[pltpu.VMEM((2, rows) + tile_shape, F32), pltpu.SemaphoreType.DMA((2,))],
        ),
        out_shape=jax.ShapeDtypeStruct((n_rows, dim), BF16),
        compiler_params=_cparams(("arbitrary",)),
        name="moe_dispatch",
    )(src_rows, hn)


def _expert_kernel(te_ref, nu_ref, x_ref, wg_ref, wu_ref, wd_ref, o_ref, acc_ref):
    del te_ref
    i = pl.program_id(0)
    j = pl.program_id(1)
    used = i < nu_ref[0]

    @pl.when(j == 0)
    def _():
        acc_ref[...] = jnp.zeros_like(acc_ref)

    @pl.when(used)
    def _():
        x = x_ref[...]
        gate = _dot(x, wg_ref[...])
        up = _dot(x, wu_ref[...])
        act = (gate * jax.nn.sigmoid(gate) * up).astype(BF16)
        acc_ref[...] += _dot(act, wd_ref[...])

    @pl.when(j == pl.num_programs(1) - 1)
    def _():
        for c in range(o_ref.shape[1]):
            o_ref[:, c, :] = acc_ref[:, c * LANES:(c + 1) * LANES]


def _expert_tiles(ff):
    for tf in (1024, 512, 256, 128):
        if ff % tf == 0:
            return tf
    return ff


def _experts(xs, tile_expert, n_used, w_gate, w_up, w_down, tm):
    n_rows, dim = xs.shape
    ff = w_gate.shape[2]
    tf = _expert_tiles(ff)
    nj = ff // tf

    def jj(i, j, nu):
        return jnp.where(i < nu[0], j, nj - 1)

    return pl.pallas_call(
        _expert_kernel,
        grid_spec=pltpu.PrefetchScalarGridSpec(
            num_scalar_prefetch=2,
            grid=(n_rows // tm, nj),
            in_specs=[pl.BlockSpec((tm, dim), lambda i, j, te, nu: (i, 0)),
                      pl.BlockSpec((None, dim, tf), lambda i, j, te, nu: (te[i], 0, jj(i, j, nu))),
                      pl.BlockSpec((None, dim, tf), lambda i, j, te, nu: (te[i], 0, jj(i, j, nu))),
                      pl.BlockSpec((None, tf, dim), lambda i, j, te, nu: (te[i], jj(i, j, nu), 0))],
            out_specs=pl.BlockSpec((tm, dim // LANES, LANES), lambda i, j, te, nu: (i, 0, 0)),
            scratch_shapes=[pltpu.VMEM((tm, dim), F32)],
        ),
        out_shape=jax.ShapeDtypeStruct((n_rows, dim // LANES, LANES), F32),
        compiler_params=_cparams(("arbitrary", "arbitrary")),
        name="moe_experts",
    )(tile_expert, n_used, xs, w_gate.astype(BF16), w_up.astype(BF16), w_down.astype(BF16))


def _combine_kernel(slots_ref, h_ref, route_ref, g_ref, y_ref, o_ref, buf_ref, sem, *, rows):
    i = pl.program_id(0)
    n = pl.num_programs(0)

    def gather(step, k):
        slot = step % 2
        return _row_gather(y_ref, buf_ref.at[slot, k], sem.at[slot],
                           lambda t: slots_ref[TOP_K * (step * rows + t) + k], rows)

    @pl.when(i == 0)
    def _():
        for k in range(TOP_K):
            gather(i, k)[0]()

    @pl.when(i + 1 < n)
    def _():
        for k in range(TOP_K):
            gather(i + 1, k)[0]()

    for k in range(TOP_K):
        gather(i, k)[1]()
    slot = i % 2
    route = route_ref[...]
    g1 = route[:, ROUTE_G1:ROUTE_G1 + 1]
    g2 = route[:, ROUTE_G2:ROUTE_G2 + 1]
    dim = h_ref.shape[1]
    ss = jnp.zeros((rows, 1), F32)
    for c in range(dim // LANES):
        cs = slice(c * LANES, (c + 1) * LANES)
        hc = h_ref[:, cs] + g1 * buf_ref[slot, 0, :, c, :] + g2 * buf_ref[slot, 1, :, c, :]
        o_ref[:, cs] = hc
        ss = ss + jnp.sum(hc * hc, axis=-1, keepdims=True)
    o_ref[...] = o_ref[...] * lax.rsqrt(ss * (1.0 / dim) + RMS_EPS) * g_ref[...]


def _combine(h2d, route, slots, y, g):
    t, dim = h2d.shape
    tile_shape = y.shape[1:]
    rows = min(t, 256)
    return pl.pallas_call(
        functools.partial(_combine_kernel, rows=rows),
        grid_spec=pltpu.PrefetchScalarGridSpec(
            num_scalar_prefetch=1,
            grid=(t // rows,),
            in_specs=[pl.BlockSpec((rows, dim), lambda i, s: (i, 0)),
                      pl.BlockSpec((rows, LANES), lambda i, s: (i, 0)),
                      pl.BlockSpec((1, dim), lambda i, s: (0, 0)),
                      pl.BlockSpec(memory_space=pl.ANY)],
            out_specs=pl.BlockSpec((rows, dim), lambda i, s: (i, 0)),
            scratch_shapes=[pltpu.VMEM((2, TOP_K, rows) + tile_shape, F32), pltpu.SemaphoreType.DMA((2,))],
        ),
        out_shape=jax.ShapeDtypeStruct((t, dim), F32),
        compiler_params=_cparams(("arbitrary",)),
        name="moe_combine",
    )(slots, h2d, route, g.reshape(1, dim), y)


def _moe_layer(h2d, norm_g, router_w, w_gate, w_up, w_down, final_g):
    t, dim = h2d.shape
    n_exp = router_w.shape[1]
    tm = min(t, 512)
    hn, route, counts = _router(h2d, norm_g, router_w)
    counts = counts[0, :n_exp].astype(jnp.int32)
    tiles = (counts + tm - 1) // tm
    tile_end = jnp.cumsum(tiles)
    row_start = (tile_end - tiles) * tm
    e12 = route[:, ROUTE_E1:ROUTE_E2 + 1].astype(jnp.int32)
    r12 = route[:, ROUTE_R1:ROUTE_R2 + 1].astype(jnp.int32)
    slots = (row_start[e12] + r12).reshape(-1)
    n_tiles = TOP_K * t // tm + n_exp
    n_used = tile_end[-1:]
    tile_ids = jnp.minimum(jnp.arange(n_tiles, dtype=jnp.int32), n_used[0] - 1)
    tile_expert = jnp.sum(tile_ids[:, None] >= tile_end[None, :], axis=1).astype(jnp.int32)
    token = jnp.repeat(jnp.arange(t, dtype=jnp.int32), TOP_K)
    src_rows = jnp.zeros((n_tiles * tm,), jnp.int32).at[slots].set(token)
    xs = _dispatch(hn, src_rows)
    y = _experts(xs, tile_expert, n_used, w_gate, w_up, w_down, tm)
    return _combine(h2d, route, slots, y, final_g)


def kernel(x, l0_norm_mix, l0_pool_w, l0_pool_scale, l0_norm_ffn, l0_ffn_w_gate, l0_ffn_w_up, l0_ffn_w_down, l1_norm_mix, l1_mix, l1_w_r, l1_w_k, l1_w_v, l1_w_o, l1_decay_w0, l1_decay_w1, l1_decay_w2, l1_iclr_a0, l1_iclr_a1, l1_iclr_a2, l1_gate_g1, l1_gate_g2, l1_k_k, l1_k_a, l1_r_k, l1_lnx_w, l1_lnx_b, l1_norm_ffn, l1_router, l1_moe_w_gate, l1_moe_w_up, l1_moe_w_down, final_norm):
    bsz, seq, dim = x.shape
    t = bsz * seq
    h = _pool_layer(x, l0_norm_mix, l0_pool_w, l0_pool_scale)
    h = _ffn_layer(h.reshape(t, dim), l0_norm_ffn, l0_ffn_w_gate, l0_ffn_w_up, l0_ffn_w_down)
    xr, xw, xk, xv, xa, xg = [m.reshape(t, dim) for m in _rwkv_prep(h.reshape(bsz, seq, dim), l1_norm_mix, l1_mix)]
    r = _proj(xr, l1_w_r)
    k, kk = _proj(xk, l1_w_k, kk_scale=l1_k_k)
    v = _proj(xv, l1_w_v)
    lw = _lora(xw, l1_decay_w1, l1_decay_w2, l1_decay_w0, "decay")
    a = _lora(xa, l1_iclr_a1, l1_iclr_a2, l1_iclr_a0, "iclr")
    g = _lora(xg, l1_gate_g1[None], l1_gate_g2[None], jnp.zeros((1, dim), F32), "gate")[0]
    b3 = lambda z: z.reshape(bsz, seq, dim)
    b4 = lambda z: z.reshape(2, bsz, seq, dim)
    y, (moe_wg, moe_wu, moe_wd) = _wkv(b3(r), b3(k), b3(v), b3(kk), b4(lw), b4(a), l1_k_a,
                                       cast_weights=(l1_moe_w_gate, l1_moe_w_up, l1_moe_w_down))
    h = _rwkv_out(y.reshape(2, t, dim), r, k, v, a, g, h, l1_k_a, l1_r_k, l1_lnx_w, l1_lnx_b, l1_w_o)
    out = _moe_layer(h, l1_norm_ffn, l1_router, moe_wg, moe_wu, moe_wd, final_norm)
    return out.reshape(bsz, seq, dim)
```

```python
import functools

import jax
import jax.numpy as jnp
from jax import lax
from jax.experimental import pallas as pl
from jax.experimental.pallas import tpu as pltpu

F32 = jnp.float32
BF16 = jnp.bfloat16

HEAD_SIZE = 64
LANES = 128
POOL_WINDOWS = (2, 4, 8, 16)
POOL_HALO = 8
TOP_K = 2
RMS_EPS = 1e-6
LNX_EPS = 64e-5
CHUNK = 64
VMEM_LIMIT_BYTES = 56 * 1024 * 1024


def _cparams(sem):
    return pltpu.CompilerParams(dimension_semantics=sem, vmem_limit_bytes=VMEM_LIMIT_BYTES)


def _dot(a, b):
    return jnp.dot(a, b, preferred_element_type=F32)


def _dot_nt(a, b):
    return lax.dot_general(a, b, (((1,), (1,)), ((), ())), preferred_element_type=F32)


def _rms(x, g):
    return x * lax.rsqrt(jnp.mean(x * x, axis=-1, keepdims=True) + RMS_EPS) * g


def _split3(x):
    hi = x.astype(BF16)
    r1 = x - hi.astype(F32)
    mid = r1.astype(BF16)
    lo = (r1 - mid.astype(F32)).astype(BF16)
    return hi, mid, lo


def _head_ones():
    r = lax.broadcasted_iota(jnp.int32, (LANES, LANES), 0) // HEAD_SIZE
    c = lax.broadcasted_iota(jnp.int32, (LANES, LANES), 1) // HEAD_SIZE
    return (r == c).astype(BF16)


def _head_sum(x, ones_bd):
    hi, mid, _ = _split3(x)
    return _dot(hi, ones_bd) + _dot(mid, ones_bd)


def _pool_kernel(x_ref, xp_ref, xn_ref, g_ref, pw_ref, ps_ref, o_ref, ext_ref, *, seq, ts, cg):
    i = pl.program_id(1)
    nt = pl.num_programs(1)
    g = g_ref[...]
    x = x_ref[...]
    hn = _rms(x, g)
    has_prev = (i > 0).astype(F32)
    has_next = (i < nt - 1).astype(F32)
    ext_ref[0:POOL_HALO, :] = _rms(xp_ref[...], g) * has_prev
    ext_ref[POOL_HALO:POOL_HALO + ts, :] = hn
    ext_ref[POOL_HALO + ts:POOL_HALO + ts + POOL_HALO, :] = _rms(xn_ref[...], g) * has_next
    pos = i * ts + lax.broadcasted_iota(jnp.int32, (ts, 1), 0)
    for gi, w in enumerate(POOL_WINDOWS):
        cs = slice(gi * cg, (gi + 1) * cg)
        lo_off = -(w // 2)
        hi_off = w - w // 2 - 1
        tot = None
        for o in range(lo_off, hi_off + 1):
            part = ext_ref[POOL_HALO + o:POOL_HALO + o + ts, cs]
            tot = part if tot is None else tot + part
        lo = jnp.maximum(pos + lo_off, 0)
        hi = jnp.minimum(pos + hi_off + 1, seq)
        cnt = (hi - lo).astype(F32)
        pooled = tot / cnt - hn[:, cs]
        mixed = _dot(pooled.astype(BF16), pw_ref[gi])
        o_ref[:, cs] = x[:, cs] + mixed * ps_ref[:, cs]


def _pool_layer(x, g, pool_w, pool_scale):
    bsz, seq, dim = x.shape
    ng = len(POOL_WINDOWS)
    cg = dim // ng
    ts = min(seq, 512)
    nt = seq // ts
    hb = ts // POOL_HALO
    nhb = seq // POOL_HALO
    kern = functools.partial(_pool_kernel, seq=seq, ts=ts, cg=cg)
    return pl.pallas_call(
        kern,
        grid=(bsz, nt),
        in_specs=[
            pl.BlockSpec((None, ts, dim), lambda b, i: (b, i, 0)),
            pl.BlockSpec((None, POOL_HALO, dim), lambda b, i: (b, jnp.maximum(i * hb - 1, 0), 0)),
            pl.BlockSpec((None, POOL_HALO, dim), lambda b, i: (b, jnp.minimum((i + 1) * hb, nhb - 1), 0)),
            pl.BlockSpec((1, dim), lambda b, i: (0, 0)),
            pl.BlockSpec((ng, cg, cg), lambda b, i: (0, 0, 0)),
            pl.BlockSpec((1, dim), lambda b, i: (0, 0)),
        ],
        out_specs=pl.BlockSpec((None, ts, dim), lambda b, i: (b, i, 0)),
        out_shape=jax.ShapeDtypeStruct(x.shape, F32),
        scratch_shapes=[pltpu.VMEM((ts + 2 * POOL_HALO, dim), F32)],
        compiler_params=_cparams(("parallel", "arbitrary")),
        name="pool_layer",
    )(x, x, x, g.reshape(1, dim), pool_w.astype(BF16), pool_scale.reshape(1, dim))


def _ffn_kernel(x_ref, g_ref, wg_ref, wu_ref, wd_ref, o_ref, hn_ref):
    j = pl.program_id(1)

    @pl.when(j == 0)
    def _():
        x = x_ref[...]
        hn_ref[...] = _rms(x, g_ref[...]).astype(BF16)
        o_ref[...] = x

    hn = hn_ref[...]
    gate = _dot(hn, wg_ref[...])
    up = _dot(hn, wu_ref[...])
    act = (gate * jax.nn.sigmoid(gate) * up).astype(BF16)
    o_ref[...] += _dot(act, wd_ref[...])


def _ffn_layer(x2d, g, w_gate, w_up, w_down):
    t, dim = x2d.shape
    ff = w_gate.shape[1]
    tm = min(t, 512)
    tf = 512 if ff % 512 == 0 else ff
    return pl.pallas_call(
        _ffn_kernel,
        grid=(t // tm, ff // tf),
        in_specs=[
            pl.BlockSpec((tm, dim), lambda i, j: (i, 0)),
            pl.BlockSpec((1, dim), lambda i, j: (0, 0)),
            pl.BlockSpec((None, dim, tf), lambda i, j: (j, 0, 0)),
            pl.BlockSpec((None, dim, tf), lambda i, j: (j, 0, 0)),
            pl.BlockSpec((tf, dim), lambda i, j: (j, 0)),
        ],
        out_specs=pl.BlockSpec((tm, dim), lambda i, j: (i, 0)),
        out_shape=jax.ShapeDtypeStruct((t, dim), F32),
        scratch_shapes=[pltpu.VMEM((tm, dim), BF16)],
        compiler_params=_cparams(("parallel", "arbitrary")),
        name="ffn_dense",
    )(x2d, g.reshape(1, dim), _col_tiled(w_gate.astype(BF16), tf), _col_tiled(w_up.astype(BF16), tf),
      w_down.astype(BF16))


def _prep_kernel(x_ref, xp_ref, xn_ref, g_ref, mix_ref, *o_refs, ts):
    i = pl.program_id(1)
    nt = pl.num_programs(1)
    g = g_ref[...]
    hn = _rms(x_ref[...], g)
    prev_row = _rms(xp_ref[POOL_HALO - 1:POOL_HALO, :], g) * (i > 0).astype(F32)
    next_row = _rms(xn_ref[0:1, :], g) * (i < nt - 1).astype(F32)
    row = lax.broadcasted_iota(jnp.int32, (ts, 1), 0)
    prev = jnp.where(row == 0, prev_row, pltpu.roll(hn, 1, axis=0))
    nxt = jnp.where(row == ts - 1, next_row, pltpu.roll(hn, ts - 1, axis=0))
    xx = 0.5 * (prev + nxt) - hn
    for n, o_ref in enumerate(o_refs):
        o_ref[...] = (hn + xx * mix_ref[n:n + 1, :]).astype(BF16)


def _rwkv_prep(h, g, mix):
    bsz, seq, dim = h.shape
    nmix = mix.shape[0]
    ts = min(seq, 512)
    nt = seq // ts
    hb = ts // POOL_HALO
    nhb = seq // POOL_HALO
    spec = pl.BlockSpec((None, ts, dim), lambda b, i: (b, i, 0))
    return pl.pallas_call(
        functools.partial(_prep_kernel, ts=ts),
        grid=(bsz, nt),
        in_specs=[
            spec,
            pl.BlockSpec((None, POOL_HALO, dim), lambda b, i: (b, jnp.maximum(i * hb - 1, 0), 0)),
            pl.BlockSpec((None, POOL_HALO, dim), lambda b, i: (b, jnp.minimum((i + 1) * hb, nhb - 1), 0)),
            pl.BlockSpec((1, dim), lambda b, i: (0, 0)),
            pl.BlockSpec((nmix, dim), lambda b, i: (0, 0)),
        ],
        out_specs=[spec] * nmix,
        out_shape=[jax.ShapeDtypeStruct(h.shape, BF16)] * nmix,
        compiler_params=_cparams(("parallel", "arbitrary")),
        name="rwkv_prep",
    )(h, h, h, g.reshape(1, dim), mix)


def _mm_kernel(x_ref, w_ref, o_ref):
    o_ref[...] = _dot(x_ref[...], w_ref[...])


def _mm_key_kernel(x_ref, w_ref, kk_scale_ref, k_ref, kk_ref):
    k = _dot(x_ref[...], w_ref[...])
    k_ref[...] = k
    ones_bd = _head_ones()
    for c in range(k.shape[1] // LANES):
        cs = slice(c * LANES, (c + 1) * LANES)
        kq = k[:, cs] * kk_scale_ref[:, cs]
        ss = _head_sum(kq * kq, ones_bd)
        kk_ref[:, cs] = kq * lax.rsqrt(jnp.maximum(ss, 1e-24))


def _proj(x2d, w, kk_scale=None):
    t, kdim = x2d.shape
    n = w.shape[1]
    tm = min(t, 1024)
    tn = min(n, 1024)
    x_spec = pl.BlockSpec((tm, kdim), lambda i, j: (i, 0))
    w_spec = pl.BlockSpec((kdim, tn), lambda i, j: (0, j))
    o_spec = pl.BlockSpec((tm, tn), lambda i, j: (i, j))
    o_shape = jax.ShapeDtypeStruct((t, n), F32)
    if kk_scale is None:
        return pl.pallas_call(
            _mm_kernel, grid=(t // tm, n // tn), in_specs=[x_spec, w_spec], out_specs=o_spec,
            out_shape=o_shape, compiler_params=_cparams(("parallel", "arbitrary")), name="rwkv_proj",
        )(x2d, w.astype(BF16))
    return pl.pallas_call(
        _mm_key_kernel, grid=(t // tm, n // tn),
        in_specs=[x_spec, w_spec, pl.BlockSpec((1, tn), lambda i, j: (0, j))],
        out_specs=[o_spec, o_spec], out_shape=[o_shape, o_shape],
        compiler_params=_cparams(("parallel", "arbitrary")), name="rwkv_proj_key",
    )(x2d, w.astype(BF16), kk_scale.reshape(1, n))


def _lora_kernel(x_ref, w1_ref, w2_ref, b_ref, o_ref, *, mode, hid):
    h = _dot(x_ref[...], w1_ref[...])
    if mode == "decay":
        h = jnp.tanh(h)
    elif mode == "gate":
        h = jax.nn.sigmoid(h)
    h = h.astype(BF16)
    for e in range(o_ref.shape[0]):
        z = _dot(h[:, e * hid:(e + 1) * hid], w2_ref[e])
        if mode == "decay":
            z = -jax.nn.sigmoid(b_ref[e] + z) * jnp.exp(F32(-0.5))
        elif mode == "iclr":
            z = jax.nn.sigmoid(b_ref[e] + z)
        o_ref[e] = z


def _lora(x2d, w1, w2, bias, mode):
    t, dim = x2d.shape
    ne, _, rank = w1.shape
    hid = -(-rank // LANES) * LANES
    w1p = jnp.pad(w1, ((0, 0), (0, 0), (0, hid - rank)))
    w1p = jnp.transpose(w1p, (1, 0, 2)).reshape(dim, ne * hid).astype(BF16)
    w2p = jnp.pad(w2, ((0, 0), (0, hid - rank), (0, 0))).astype(BF16)
    tm = min(t, 512)
    return pl.pallas_call(
        functools.partial(_lora_kernel, mode=mode, hid=hid),
        grid=(t // tm,),
        in_specs=[
            pl.BlockSpec((tm, dim), lambda i: (i, 0)),
            pl.BlockSpec((dim, ne * hid), lambda i: (0, 0)),
            pl.BlockSpec((ne, hid, dim), lambda i: (0, 0, 0)),
            pl.BlockSpec((ne, 1, dim), lambda i: (0, 0, 0)),
        ],
        out_specs=pl.BlockSpec((ne, tm, dim), lambda i: (0, i, 0)),
        out_shape=jax.ShapeDtypeStruct((ne, t, dim), F32),
        compiler_params=_cparams(("parallel",)),
        name="rwkv_lora_" + mode,
    )(x2d, w1p, w2p, bias.reshape(ne, 1, dim))


def _pair_blockdiag(x):
    lane = lax.broadcasted_iota(jnp.int32, x.shape, 1)
    top = jnp.where(lane < HEAD_SIZE, x, 0.0)
    bot = jnp.where(lane >= HEAD_SIZE, x, 0.0)
    return jnp.concatenate([top, bot], axis=0).astype(BF16)


def _pair_dot(a, b):
    return _dot(a.astype(BF16), _pair_blockdiag(b))


def _wkv_kernel(r_ref, k_ref, v_ref, kk_ref, lw_ref, a_ref, ka_ref, *rest):
    n_cast = (len(rest) - 2) // 2
    y_ref, s_ref = rest[n_cast], rest[-1]
    for w_ref, wo_ref in zip(rest[:n_cast], rest[n_cast + 1:-1]):
        if len(wo_ref.shape) == 3:
            tf = wo_ref.shape[2]
            for j in range(wo_ref.shape[0]):
                wo_ref[j] = w_ref[:, j * tf:(j + 1) * tf].astype(BF16)
        else:
            wo_ref[...] = w_ref[...].astype(BF16)
    d = pl.program_id(0)
    sgn = 1 - 2 * d
    fwd_f = (1 - d).astype(F32)
    L = CHUNK

    @pl.when(pl.program_id(2) == 0)
    def _():
        s_ref[...] = jnp.zeros_like(s_ref)

    t_sq = lax.broadcasted_iota(jnp.int32, (L, L), 0)
    s_sq = lax.broadcasted_iota(jnp.int32, (L, L), 1)
    tri = (((t_sq - s_sq) * sgn) >= 0).astype(BF16)
    t_p = lax.broadcasted_iota(jnp.int32, (L, LANES), 0)
    s_p = lax.broadcasted_iota(jnp.int32, (L, LANES), 1) % HEAD_SIZE
    rel = (t_p - s_p) * sgn
    m_strict = rel > 0
    m_read = (rel + (1 - d)) > 0
    eye = (t_p == s_p).astype(F32)
    row_h = lax.broadcasted_iota(jnp.int32, (LANES, LANES), 0) // HEAD_SIZE
    col_h = lax.broadcasted_iota(jnp.int32, (LANES, LANES), 1) // HEAD_SIZE
    same_head = row_h == col_h
    n_pairs = s_ref.shape[0]
    n_sub = r_ref.shape[0] // L
    sub_rows = [pl.ds(pl.multiple_of((i + d * (n_sub - 1 - 2 * i)) * L, L), L) for i in range(n_sub)]
    cols = [slice(p * LANES, (p + 1) * LANES) for p in range(n_pairs)]
    rows = [rw for rw in sub_rows for _ in cols]
    ccol = [cs for _ in sub_rows for cs in cols]

    def each(fn, *lists):
        return [fn(*args) for args in zip(*lists)]

    def load(ref):
        return [ref[rw, cs] for rw, cs in zip(rows, ccol)]

    def cumulative(lw):
        m = _dot(tri, jnp.concatenate(_split3(lw), axis=1))
        return m[:, :LANES] + m[:, LANES:2 * LANES] + m[:, 2 * LANES:]

    def halves(m):
        return m[:, :LANES], m[:, LANES:]

    def lanes2(x, y):
        return jnp.concatenate([_pair_blockdiag(x), _pair_blockdiag(y)], axis=1)

    lw = load(lw_ref)
    c_incl = each(cumulative, lw)
    c_excl = each(lambda c, w: c - w, c_incl, lw)
    tot = each(lambda w: jnp.sum(w, axis=0, keepdims=True), lw)
    kk = load(kk_ref)
    a = load(a_ref)
    beta = each(lambda x, y: x * y, kk, a)
    kt = each(lambda x, a_p, cs: x * (1.0 + (a_p - 1.0) * ka_ref[:, cs]), load(k_ref), a, ccol)
    ah = each(lambda x, c: -x * jnp.exp(c), kk, c_excl)
    rh = each(lambda x, c, w: x * jnp.exp(c + fwd_f * w), load(r_ref), c_excl, lw)
    e_neg = each(lambda c: jnp.exp(-c), c_incl)
    lhs = each(lambda x, y: jnp.concatenate([x, y], axis=0).astype(BF16), ah, rh)
    a_bk = each(lambda l, b, k_, e: _dot_nt(l, jnp.concatenate([_pair_blockdiag(b * e), _pair_blockdiag(k_ * e)],
                                                               axis=0)), lhs, beta, kt, e_neg)
    n_ab = each(lambda m: jnp.where(m_strict, m[:L, :LANES], 0.0), a_bk)
    n_ak = each(lambda m: jnp.where(m_strict, m[:L, LANES:], 0.0), a_bk)
    n_rb = each(lambda m: jnp.where(m_read, m[L:, :LANES], 0.0), a_bk)
    n_rk = each(lambda m: jnp.where(m_read, m[L:, LANES:], 0.0), a_bk)
    v = load(v_ref)
    xy = each(lambda x, y, v_p: _dot(jnp.concatenate([x, y], axis=0).astype(BF16), _pair_blockdiag(v_p)),
              n_ak, n_rk, v)
    tm = each(lambda n: eye + n, n_ab)
    pw = each(_pair_dot, n_ab, n_ab)
    for _ in range(1, L.bit_length() - 2):
        st = each(lambda q, t: _dot(jnp.concatenate([q, t], axis=0).astype(BF16), _pair_blockdiag(q)), pw, tm)
        tm = each(lambda t, m: t + m[L:], tm, st)
        pw = each(lambda m: m[:L], st)
    tm = each(lambda t, q: t + _pair_dot(t, q), tm, pw)
    at_ui = each(lambda t, x, m: halves(_dot(t.astype(BF16), lanes2(x, m[:L]))), tm, ah, xy)
    rt_yi = each(lambda n, au: halves(_dot(n.astype(BF16), lanes2(au[0], au[1]))), n_rb, at_ui)
    rt = each(lambda x, ry: x + ry[0], rh, rt_yi)
    yi = each(lambda ry, m: ry[1] + m[L:], rt_yi, xy)
    e_bar = each(lambda t, c: jnp.exp(t - c), tot, c_incl)
    rhs = each(lambda b, k_, e: jnp.concatenate([b * e, k_ * e], axis=0).astype(BF16), beta, kt, e_bar)
    for i, rw in enumerate(sub_rows):
        ch = slice(i * n_pairs, (i + 1) * n_pairs)
        s = [s_ref[p] for p in range(n_pairs)]
        uy = each(lambda au, x, s_p: _dot_nt(jnp.concatenate([au[0], x], axis=0).astype(BF16), s_p.astype(BF16)),
                  at_ui[ch], rt[ch], s)
        for cs, uy_p, yi_p in zip(cols, uy, yi[ch]):
            y_ref[rw, cs] = uy_p[L:] + yi_p
        uv_t = each(lambda uy_p, au, v_p: jnp.concatenate([uy_p[:L] + au[1], v_p], axis=0).T.astype(BF16),
                    uy, at_ui[ch], v[ch])
        upd = each(_dot, uv_t, rhs[ch])
        for p in range(n_pairs):
            s_ref[p] = s[p] * jnp.exp(tot[ch][p]) + jnp.where(same_head, upd[p], 0.0)


BF16_SUBLANES = 16
WKV_CHUNKS_PER_STEP = 2


def _col_tiled_shape(shape, ct):
    return shape[:-2] + (shape[-1] // ct, shape[-2], ct)


def _col_tiled(w, ct):
    if ct is None:
        return w
    lead = w.shape[:-2]
    kdim, n = w.shape[-2:]
    return jnp.swapaxes(w.reshape(lead + (kdim, n // ct, ct)), -3, -2)


def _wkv(r, k, v, kk, lw, a, k_a, cast_weights=(), col_tiles=()):
    bsz, seq, dim = r.shape
    sub = WKV_CHUNKS_PER_STEP if (seq // CHUNK) % WKV_CHUNKS_PER_STEP == 0 else 1
    rows = sub * CHUNK
    nc = seq // rows
    steps = 2 * bsz * nc

    def cidx(d, c):
        return c + d * (nc - 1 - 2 * c)

    def step(d, b, c):
        return (d * bsz + b) * nc + c

    col_tiles = tuple(col_tiles) + (None,) * (len(cast_weights) - len(col_tiles))
    flat = [w.reshape(-1, w.shape[-1]) for w in cast_weights]
    slab = [w.shape[0] // steps for w in flat]
    in_kernel = all(f.shape[0] % steps == 0 and s % BF16_SUBLANES == 0 and w.shape[1] % s == 0
                    for f, s, w in zip(flat, slab, cast_weights))
    if not in_kernel:
        flat, slab = [], []
    w_specs = [pl.BlockSpec((s, w.shape[1]), lambda d, b, c: (step(d, b, c), 0)) for w, s in zip(flat, slab)]
    wo_specs, wo_shapes = [], []
    for w, f, s, ct in zip(cast_weights, flat, slab, col_tiles):
        if ct is None:
            wo_specs.append(pl.BlockSpec((s, f.shape[1]), lambda d, b, c: (step(d, b, c), 0)))
            wo_shapes.append(jax.ShapeDtypeStruct(f.shape, BF16))
        else:
            per = w.shape[1] // s
            wo_specs.append(pl.BlockSpec((None, w.shape[2] // ct, s, ct),
                                         lambda d, b, c, per=per: (step(d, b, c) // per, 0, step(d, b, c) % per, 0)))
            wo_shapes.append(jax.ShapeDtypeStruct(_col_tiled_shape(w.shape, ct), BF16))
    tok = pl.BlockSpec((None, rows, dim), lambda d, b, c: (b, cidx(d, c), 0))
    dirtok = pl.BlockSpec((None, None, rows, dim), lambda d, b, c: (d, b, cidx(d, c), 0))
    outs = pl.pallas_call(
        _wkv_kernel,
        grid=(2, bsz, nc),
        in_specs=[tok, tok, tok, tok, dirtok, dirtok, pl.BlockSpec((1, dim), lambda d, b, c: (0, 0))] + w_specs,
        out_specs=[dirtok] + wo_specs,
        out_shape=[jax.ShapeDtypeStruct((2, bsz, seq, dim), F32)] + wo_shapes,
        scratch_shapes=[pltpu.VMEM((dim // LANES, LANES, LANES), F32)],
        compiler_params=_cparams(("arbitrary", "arbitrary", "arbitrary")),
        name="wkv_scan",
    )(r, k, v, kk, lw, a, k_a.reshape(1, dim), *flat)
    if in_kernel:
        return outs[0], [o if ct is not None else o.reshape(w.shape)
                         for o, w, ct in zip(outs[1:], cast_weights, col_tiles)]
    return outs[0], [_col_tiled(w.astype(BF16), ct) for w, ct in zip(cast_weights, col_tiles)]


def _rwkv_out_kernel(yf_ref, yb_ref, r_ref, k_ref, v_ref, a_ref, g_ref, h_ref,
                     ka_ref, rk_ref, lw_ref, lb_ref, wo_ref, o_ref, z_ref):
    ones_bd = _head_ones()
    inv_n = 1.0 / HEAD_SIZE
    for c in range(h_ref.shape[1] // LANES):
        cs = slice(c * LANES, (c + 1) * LANES)
        y = yf_ref[:, cs] + yb_ref[:, cs]
        mu = _head_sum(y, ones_bd) * inv_n
        yc = y - mu
        var = _head_sum(yc * yc, ones_bd) * inv_n
        yn = yc * lax.rsqrt(var + LNX_EPS) * lw_ref[:, cs] + lb_ref[:, cs]
        kd = k_ref[:, cs] * (1.0 + (a_ref[:, cs] - 1.0) * ka_ref[:, cs])
        bonus = _head_sum(r_ref[:, cs] * kd * rk_ref[:, cs], ones_bd) * v_ref[:, cs]
        z_ref[:, cs] = ((yn + bonus) * g_ref[:, cs]).astype(BF16)
    o_ref[...] = h_ref[...] + _dot(z_ref[...], wo_ref[...])


def _rwkv_out(y, r, k, v, a, g, h2d, k_a, r_k, lnx_w, lnx_b, w_o):
    t, dim = h2d.shape
    tm = min(t, 256)
    tok = pl.BlockSpec((tm, dim), lambda i: (i, 0))
    vec = pl.BlockSpec((1, dim), lambda i: (0, 0))
    return pl.pallas_call(
        _rwkv_out_kernel,
        grid=(t // tm,),
        in_specs=[pl.BlockSpec((None, tm, dim), lambda i: (0, i, 0)),
                  pl.BlockSpec((None, tm, dim), lambda i: (1, i, 0)),
                  tok, tok, tok,
                  pl.BlockSpec((None, tm, dim), lambda i: (0, i, 0)),
                  tok, tok, vec, vec, vec, vec,
                  pl.BlockSpec((dim, dim), lambda i: (0, 0))],
        out_specs=tok,
        out_shape=jax.ShapeDtypeStruct((t, dim), F32),
        scratch_shapes=[pltpu.VMEM((tm, dim), BF16)],
        compiler_params=_cparams(("parallel",)),
        name="rwkv_out",
    )(y, y, r, k, v, a, g, h2d, k_a.reshape(1, dim), r_k.reshape(1, dim),
      lnx_w.reshape(1, dim), lnx_b.reshape(1, dim), w_o.astype(BF16))


ROUTE_E1, ROUTE_E2, ROUTE_G1, ROUTE_G2, ROUTE_R1, ROUTE_R2 = range(6)


def _router_kernel(x_ref, g_ref, w_ref, hn_ref, route_ref, cnt_ref, *, n_exp, tm):
    i = pl.program_id(0)

    @pl.when(i == 0)
    def _():
        cnt_ref[...] = jnp.zeros_like(cnt_ref)

    hn = _rms(x_ref[...], g_ref[...])
    for c in range(hn_ref.shape[1]):
        hn_ref[:, c, :] = hn[:, c * LANES:(c + 1) * LANES]
    x_hi, x_mid, x_lo = _split3(hn)
    w_hi, w_mid, w_lo = w_ref[0], w_ref[1], w_ref[2]
    logits = (_dot(x_hi, w_hi) + _dot(x_hi, w_mid) + _dot(x_mid, w_hi)
              + _dot(x_hi, w_lo) + _dot(x_mid, w_mid) + _dot(x_lo, w_hi))
    lane = lax.broadcasted_iota(jnp.int32, (tm, LANES), 1)
    neg = F32(-jnp.inf)
    lg = jnp.where(lane < n_exp, logits, neg)
    m1 = jnp.max(lg, axis=-1, keepdims=True)
    i1 = jnp.min(jnp.where(lg == m1, lane, LANES), axis=-1, keepdims=True)
    lg2 = jnp.where(lane == i1, neg, lg)
    m2 = jnp.max(lg2, axis=-1, keepdims=True)
    i2 = jnp.min(jnp.where(lg2 == m2, lane, LANES), axis=-1, keepdims=True)
    e2 = jnp.exp(m2 - m1)
    den = 1.0 + e2
    g1 = 1.0 / den
    g2 = e2 / den
    oh1 = lane == i1
    oh2 = lane == i2
    oh = oh1.astype(F32) + oh2.astype(F32)
    rr = lax.broadcasted_iota(jnp.int32, (tm, tm), 0)
    cc = lax.broadcasted_iota(jnp.int32, (tm, tm), 1)
    before = (cc < rr).astype(BF16)
    prefix = _dot(before, oh.astype(BF16)) + cnt_ref[...]
    r1 = jnp.sum(jnp.where(oh1, prefix, 0.0), axis=-1, keepdims=True)
    r2 = jnp.sum(jnp.where(oh2, prefix, 0.0), axis=-1, keepdims=True)
    cnt_ref[...] += jnp.sum(oh, axis=0, keepdims=True)
    route = jnp.zeros((tm, LANES), F32)
    for col, val in ((ROUTE_E1, i1.astype(F32)), (ROUTE_E2, i2.astype(F32)), (ROUTE_G1, g1),
                     (ROUTE_G2, g2), (ROUTE_R1, r1), (ROUTE_R2, r2)):
        route = jnp.where(lane == col, val, route)
    route_ref[...] = route


def _router(h2d, g, router_w):
    t, dim = h2d.shape
    n_exp = router_w.shape[1]
    tm = min(t, 512)
    wp = jnp.pad(router_w, ((0, 0), (0, LANES - n_exp)))
    w3 = jnp.stack(_split3(wp))
    return pl.pallas_call(
        functools.partial(_router_kernel, n_exp=n_exp, tm=tm),
        grid=(t // tm,),
        in_specs=[pl.BlockSpec((tm, dim), lambda i: (i, 0)),
                  pl.BlockSpec((1, dim), lambda i: (0, 0)),
                  pl.BlockSpec((3, dim, LANES), lambda i: (0, 0, 0))],
        out_specs=[pl.BlockSpec((tm, dim // LANES, LANES), lambda i: (i, 0, 0)),
                   pl.BlockSpec((tm, LANES), lambda i: (i, 0)),
                   pl.BlockSpec((1, LANES), lambda i: (0, 0))],
        out_shape=[jax.ShapeDtypeStruct((t, dim // LANES, LANES), F32),
                   jax.ShapeDtypeStruct((t, LANES), F32),
                   jax.ShapeDtypeStruct((1, LANES), F32)],
        compiler_params=_cparams(("arbitrary",)),
        name="moe_router",
    )(h2d, g.reshape(1, dim), w3)


def _row_gather(src_hbm, dst_vmem, sem, index_of, rows):
    def copy(t):
        return pltpu.make_async_copy(src_hbm.at[pl.ds(index_of(t), 1)], dst_vmem.at[pl.ds(t, 1)], sem)

    def start():
        lax.fori_loop(0, rows, lambda t, c: (copy(t).start(), c)[1], 0, unroll=8)

    def wait():
        for t in range(rows):
            pltpu.make_async_copy(src_hbm.at[pl.ds(0, 1)], dst_vmem.at[pl.ds(t, 1)], sem).wait()

    return start, wait


def _dispatch_kernel(src_ref, hn_ref, xs_ref, buf_ref, sem, *, rows):
    i = pl.program_id(0)
    n = pl.num_programs(0)

    def gather(step):
        slot = step % 2
        return _row_gather(hn_ref, buf_ref.at[slot], sem.at[slot], lambda t: src_ref[step * rows + t], rows)

    @pl.when(i == 0)
    def _():
        gather(i)[0]()

    @pl.when(i + 1 < n)
    def _():
        gather(i + 1)[0]()

    gather(i)[1]()
    slot = i % 2
    for c in range(buf_ref.shape[2]):
        xs_ref[:, c * LANES:(c + 1) * LANES] = buf_ref[slot, :, c, :].astype(BF16)


def _dispatch(hn, src_rows):
    tile_shape = hn.shape[1:]
    dim = tile_shape[0] * tile_shape[1]
    n_rows = src_rows.shape[0]
    rows = min(n_rows, 512)
    return pl.pallas_call(
        functools.partial(_dispatch_kernel, rows=rows),
        grid_spec=pltpu.PrefetchScalarGridSpec(
            num_scalar_prefetch=1,
            grid=(n_rows // rows,),
            in_specs=[pl.BlockSpec(memory_space=pl.ANY)],
            out_specs=pl.BlockSpec((rows, dim), lambda i, s: (i, 0)),
            scratch_shapes=[pltpu.VMEM((2, rows) + tile_shape, F32), pltpu.SemaphoreType.DMA((2,))],
        ),
        out_shape=jax.ShapeDtypeStruct((n_rows, dim), BF16),
        compiler_params=_cparams(("arbitrary",)),
        name="moe_dispatch",
    )(src_rows, hn)


def _expert_kernel(te_ref, nu_ref, x_ref, wg_ref, wu_ref, wd_ref, o_ref, acc_ref):
    del te_ref
    i = pl.program_id(0)
    j = pl.program_id(1)
    used = i < nu_ref[0]

    @pl.when(j == 0)
    def _():
        acc_ref[...] = jnp.zeros_like(acc_ref)

    @pl.when(used)
    def _():
        x = x_ref[...]
        gate = _dot(x, wg_ref[...])
        up = _dot(x, wu_ref[...])
        act = (gate * jax.nn.sigmoid(gate) * up).astype(BF16)
        acc_ref[...] += _dot(act, wd_ref[...])

    @pl.when(j == pl.num_programs(1) - 1)
    def _():
        for c in range(o_ref.shape[1]):
            o_ref[:, c, :] = acc_ref[:, c * LANES:(c + 1) * LANES]


def _expert_tiles(ff):
    for tf in (1024, 512, 256, 128):
        if ff % tf == 0:
            return tf
    return ff


def _experts(xs, tile_expert, n_used, w_gate, w_up, w_down, tm):
    n_rows, dim = xs.shape
    ff = w_down.shape[1]
    tf = _expert_tiles(ff)
    nj = ff // tf
    if w_gate.ndim == 3:
        w_gate = _col_tiled(w_gate.astype(BF16), tf)
        w_up = _col_tiled(w_up.astype(BF16), tf)

    def jj(i, j, nu):
        return jnp.where(i < nu[0], j, nj - 1)

    return pl.pallas_call(
        _expert_kernel,
        grid_spec=pltpu.PrefetchScalarGridSpec(
            num_scalar_prefetch=2,
            grid=(n_rows // tm, nj),
            in_specs=[pl.BlockSpec((tm, dim), lambda i, j, te, nu: (i, 0)),
                      pl.BlockSpec((None, None, dim, tf), lambda i, j, te, nu: (te[i], jj(i, j, nu), 0, 0)),
                      pl.BlockSpec((None, None, dim, tf), lambda i, j, te, nu: (te[i], jj(i, j, nu), 0, 0)),
                      pl.BlockSpec((None, tf, dim), lambda i, j, te, nu: (te[i], jj(i, j, nu), 0))],
            out_specs=pl.BlockSpec((tm, dim // LANES, LANES), lambda i, j, te, nu: (i, 0, 0)),
            scratch_shapes=[pltpu.VMEM((tm, dim), F32)],
        ),
        out_shape=jax.ShapeDtypeStruct((n_rows, dim // LANES, LANES), F32),
        compiler_params=_cparams(("arbitrary", "arbitrary")),
        name="moe_experts",
    )(tile_expert, n_used, xs, w_gate, w_up, w_down.astype(BF16))


def _combine_kernel(slots_ref, h_ref, route_ref, g_ref, y_ref, o_ref, buf_ref, sem, *, rows):
    i = pl.program_id(0)
    n = pl.num_programs(0)

    def gather(step, k):
        slot = step % 2
        return _row_gather(y_ref, buf_ref.at[slot, k], sem.at[slot],
                           lambda t: slots_ref[TOP_K * (step * rows + t) + k], rows)

    @pl.when(i == 0)
    def _():
        for k in range(TOP_K):
            gather(i, k)[0]()

    @pl.when(i + 1 < n)
    def _():
        for k in range(TOP_K):
            gather(i + 1, k)[0]()

    for k in range(TOP_K):
        gather(i, k)[1]()
    slot = i % 2
    route = route_ref[...]
    g1 = route[:, ROUTE_G1:ROUTE_G1 + 1]
    g2 = route[:, ROUTE_G2:ROUTE_G2 + 1]
    dim = h_ref.shape[1]
    ss = jnp.zeros((rows, 1), F32)
    for c in range(dim // LANES):
        cs = slice(c * LANES, (c + 1) * LANES)
        hc = h_ref[:, cs] + g1 * buf_ref[slot, 0, :, c, :] + g2 * buf_ref[slot, 1, :, c, :]
        o_ref[:, cs] = hc
        ss = ss + jnp.sum(hc * hc, axis=-1, keepdims=True)
    o_ref[...] = o_ref[...] * lax.rsqrt(ss * (1.0 / dim) + RMS_EPS) * g_ref[...]


def _combine(h2d, route, slots, y, g):
    t, dim = h2d.shape
    tile_shape = y.shape[1:]
    rows = min(t, 256)
    return pl.pallas_call(
        functools.partial(_combine_kernel, rows=rows),
        grid_spec=pltpu.PrefetchScalarGridSpec(
            num_scalar_prefetch=1,
            grid=(t // rows,),
            in_specs=[pl.BlockSpec((rows, dim), lambda i, s: (i, 0)),
                      pl.BlockSpec((rows, LANES), lambda i, s: (i, 0)),
                      pl.BlockSpec((1, dim), lambda i, s: (0, 0)),
                      pl.BlockSpec(memory_space=pl.ANY)],
            out_specs=pl.BlockSpec((rows, dim), lambda i, s: (i, 0)),
            scratch_shapes=[pltpu.VMEM((2, TOP_K, rows) + tile_shape, F32), pltpu.SemaphoreType.DMA((2,))],
        ),
        out_shape=jax.ShapeDtypeStruct((t, dim), F32),
        compiler_params=_cparams(("arbitrary",)),
        name="moe_combine",
    )(slots, h2d, route, g.reshape(1, dim), y)


def _moe_layer(h2d, norm_g, router_w, w_gate, w_up, w_down, final_g):
    t, dim = h2d.shape
    n_exp = router_w.shape[1]
    tm = min(t, 512)
    hn, route, counts = _router(h2d, norm_g, router_w)
    counts = counts[0, :n_exp].astype(jnp.int32)
    tiles = (counts + tm - 1) // tm
    tile_end = jnp.cumsum(tiles)
    row_start = (tile_end - tiles) * tm
    e12 = route[:, ROUTE_E1:ROUTE_E2 + 1].astype(jnp.int32)
    r12 = route[:, ROUTE_R1:ROUTE_R2 + 1].astype(jnp.int32)
    slots = (row_start[e12] + r12).reshape(-1)
    n_tiles = TOP_K * t // tm + n_exp
    n_used = tile_end[-1:]
    tile_ids = jnp.minimum(jnp.arange(n_tiles, dtype=jnp.int32), n_used[0] - 1)
    tile_expert = jnp.sum(tile_ids[:, None] >= tile_end[None, :], axis=1).astype(jnp.int32)
    token = jnp.repeat(jnp.arange(t, dtype=jnp.int32), TOP_K)
    src_rows = jnp.zeros((n_tiles * tm,), jnp.int32).at[slots].set(token)
    xs = _dispatch(hn, src_rows)
    y = _experts(xs, tile_expert, n_used, w_gate, w_up, w_down, tm)
    return _combine(h2d, route, slots, y, final_g)


def kernel(x, l0_norm_mix, l0_pool_w, l0_pool_scale, l0_norm_ffn, l0_ffn_w_gate, l0_ffn_w_up, l0_ffn_w_down, l1_norm_mix, l1_mix, l1_w_r, l1_w_k, l1_w_v, l1_w_o, l1_decay_w0, l1_decay_w1, l1_decay_w2, l1_iclr_a0, l1_iclr_a1, l1_iclr_a2, l1_gate_g1, l1_gate_g2, l1_k_k, l1_k_a, l1_r_k, l1_lnx_w, l1_lnx_b, l1_norm_ffn, l1_router, l1_moe_w_gate, l1_moe_w_up, l1_moe_w_down, final_norm):
    bsz, seq, dim = x.shape
    t = bsz * seq
    h = _pool_layer(x, l0_norm_mix, l0_pool_w, l0_pool_scale)
    h = _ffn_layer(h.reshape(t, dim), l0_norm_ffn, l0_ffn_w_gate, l0_ffn_w_up, l0_ffn_w_down)
    xr, xw, xk, xv, xa, xg = [m.reshape(t, dim) for m in _rwkv_prep(h.reshape(bsz, seq, dim), l1_norm_mix, l1_mix)]
    r = _proj(xr, l1_w_r)
    k, kk = _proj(xk, l1_w_k, kk_scale=l1_k_k)
    v = _proj(xv, l1_w_v)
    lw = _lora(xw, l1_decay_w1, l1_decay_w2, l1_decay_w0, "decay")
    a = _lora(xa, l1_iclr_a1, l1_iclr_a2, l1_iclr_a0, "iclr")
    g = _lora(xg, l1_gate_g1[None], l1_gate_g2[None], jnp.zeros((1, dim), F32), "gate")[0]
    b3 = lambda z: z.reshape(bsz, seq, dim)
    b4 = lambda z: z.reshape(2, bsz, seq, dim)
    tf_e = _expert_tiles(l1_moe_w_gate.shape[2])
    y, (moe_wg, moe_wu, moe_wd) = _wkv(b3(r), b3(k), b3(v), b3(kk), b4(lw), b4(a), l1_k_a,
                                       cast_weights=(l1_moe_w_gate, l1_moe_w_up, l1_moe_w_down),
                                       col_tiles=(tf_e, tf_e, None))
    h = _rwkv_out(y.reshape(2, t, dim), r, k, v, a, g, h, l1_k_a, l1_r_k, l1_lnx_w, l1_lnx_b, l1_w_o)
    out = _moe_layer(h, l1_norm_ffn, l1_router, moe_wg, moe_wu, moe_wd, final_norm)
    return out.reshape(bsz, seq, dim)
```

```python
import functools

import jax
import jax.numpy as jnp
from jax import lax
from jax.experimental import pallas as pl
from jax.experimental.pallas import tpu as pltpu

F32 = jnp.float32
BF16 = jnp.bfloat16

HEAD_SIZE = 64
LANES = 128
POOL_WINDOWS = (2, 4, 8, 16)
POOL_HALO = 8
TOP_K = 2
RMS_EPS = 1e-6
LNX_EPS = 64e-5
CHUNK = 64
VMEM_LIMIT_BYTES = 56 * 1024 * 1024


def _cparams(sem):
    return pltpu.CompilerParams(dimension_semantics=sem, vmem_limit_bytes=VMEM_LIMIT_BYTES)


def _dot(a, b):
    return jnp.dot(a, b, preferred_element_type=F32)


def _dot_nt(a, b):
    return lax.dot_general(a, b, (((1,), (1,)), ((), ())), preferred_element_type=F32)


def _rms(x, g):
    return x * lax.rsqrt(jnp.mean(x * x, axis=-1, keepdims=True) + RMS_EPS) * g


def _split3(x):
    hi = x.astype(BF16)
    r1 = x - hi.astype(F32)
    mid = r1.astype(BF16)
    lo = (r1 - mid.astype(F32)).astype(BF16)
    return hi, mid, lo


def _head_ones():
    r = lax.broadcasted_iota(jnp.int32, (LANES, LANES), 0) // HEAD_SIZE
    c = lax.broadcasted_iota(jnp.int32, (LANES, LANES), 1) // HEAD_SIZE
    return (r == c).astype(BF16)


def _head_sum(x, ones_bd):
    hi, mid, _ = _split3(x)
    return _dot(hi, ones_bd) + _dot(mid, ones_bd)


def _pool_kernel(x_ref, xp_ref, xn_ref, g_ref, pw_ref, ps_ref, o_ref, ext_ref, *, seq, ts, cg):
    i = pl.program_id(1)
    nt = pl.num_programs(1)
    g = g_ref[...]
    x = x_ref[...]
    hn = _rms(x, g)
    has_prev = (i > 0).astype(F32)
    has_next = (i < nt - 1).astype(F32)
    ext_ref[0:POOL_HALO, :] = _rms(xp_ref[...], g) * has_prev
    ext_ref[POOL_HALO:POOL_HALO + ts, :] = hn
    ext_ref[POOL_HALO + ts:POOL_HALO + ts + POOL_HALO, :] = _rms(xn_ref[...], g) * has_next
    pos = i * ts + lax.broadcasted_iota(jnp.int32, (ts, 1), 0)
    for gi, w in enumerate(POOL_WINDOWS):
        cs = slice(gi * cg, (gi + 1) * cg)
        lo_off = -(w // 2)
        hi_off = w - w // 2 - 1
        tot = None
        for o in range(lo_off, hi_off + 1):
            part = ext_ref[POOL_HALO + o:POOL_HALO + o + ts, cs]
            tot = part if tot is None else tot + part
        lo = jnp.maximum(pos + lo_off, 0)
        hi = jnp.minimum(pos + hi_off + 1, seq)
        cnt = (hi - lo).astype(F32)
        pooled = tot / cnt - hn[:, cs]
        mixed = _dot(pooled.astype(BF16), pw_ref[gi])
        o_ref[:, cs] = x[:, cs] + mixed * ps_ref[:, cs]


def _pool_layer(x, g, pool_w, pool_scale):
    bsz, seq, dim = x.shape
    ng = len(POOL_WINDOWS)
    cg = dim // ng
    ts = min(seq, 512)
    nt = seq // ts
    hb = ts // POOL_HALO
    nhb = seq // POOL_HALO
    kern = functools.partial(_pool_kernel, seq=seq, ts=ts, cg=cg)
    return pl.pallas_call(
        kern,
        grid=(bsz, nt),
        in_specs=[
            pl.BlockSpec((None, ts, dim), lambda b, i: (b, i, 0)),
            pl.BlockSpec((None, POOL_HALO, dim), lambda b, i: (b, jnp.maximum(i * hb - 1, 0), 0)),
            pl.BlockSpec((None, POOL_HALO, dim), lambda b, i: (b, jnp.minimum((i + 1) * hb, nhb - 1), 0)),
            pl.BlockSpec((1, dim), lambda b, i: (0, 0)),
            pl.BlockSpec((ng, cg, cg), lambda b, i: (0, 0, 0)),
            pl.BlockSpec((1, dim), lambda b, i: (0, 0)),
        ],
        out_specs=pl.BlockSpec((None, ts, dim), lambda b, i: (b, i, 0)),
        out_shape=jax.ShapeDtypeStruct(x.shape, F32),
        scratch_shapes=[pltpu.VMEM((ts + 2 * POOL_HALO, dim), F32)],
        compiler_params=_cparams(("parallel", "arbitrary")),
        name="pool_layer",
    )(x, x, x, g.reshape(1, dim), pool_w.astype(BF16), pool_scale.reshape(1, dim))


def _ffn_kernel(x_ref, g_ref, wg_ref, wu_ref, wd_ref, o_ref, hn_ref):
    j = pl.program_id(1)

    @pl.when(j == 0)
    def _():
        x = x_ref[...]
        hn_ref[...] = _rms(x, g_ref[...]).astype(BF16)
        o_ref[...] = x

    hn = hn_ref[...]
    gate = _dot(hn, wg_ref[...])
    up = _dot(hn, wu_ref[...])
    act = (gate * jax.nn.sigmoid(gate) * up).astype(BF16)
    o_ref[...] += _dot(act, wd_ref[...])


def _ffn_layer(x2d, g, w_gate, w_up, w_down):
    t, dim = x2d.shape
    ff = w_gate.shape[1]
    tm = min(t, 512)
    tf = 512 if ff % 512 == 0 else ff
    return pl.pallas_call(
        _ffn_kernel,
        grid=(t // tm, ff // tf),
        in_specs=[
            pl.BlockSpec((tm, dim), lambda i, j: (i, 0)),
            pl.BlockSpec((1, dim), lambda i, j: (0, 0)),
            pl.BlockSpec((dim, tf), lambda i, j: (0, j)),
            pl.BlockSpec((dim, tf), lambda i, j: (0, j)),
            pl.BlockSpec((tf, dim), lambda i, j: (j, 0)),
        ],
        out_specs=pl.BlockSpec((tm, dim), lambda i, j: (i, 0)),
        out_shape=jax.ShapeDtypeStruct((t, dim), F32),
        scratch_shapes=[pltpu.VMEM((tm, dim), BF16)],
        compiler_params=_cparams(("parallel", "arbitrary")),
        name="ffn_dense",
    )(x2d, g.reshape(1, dim), w_gate.astype(BF16), w_up.astype(BF16), w_down.astype(BF16))


MIX_R, MIX_W, MIX_K, MIX_V, MIX_A, MIX_G = range(6)
LORA_ACT = {"decay": jnp.tanh, "iclr": lambda z: z, "gate": jax.nn.sigmoid}


def _prep_kernel(x_ref, xp_ref, xn_ref, g_ref, mix_ref, ww_ref, wa_ref, wg_ref,
                 xr_ref, xk_ref, xv_ref, hw_ref, ha_ref, hg_ref, *, ts):
    i = pl.program_id(1)
    nt = pl.num_programs(1)
    g = g_ref[...]
    hn = _rms(x_ref[...], g)
    prev_row = _rms(xp_ref[POOL_HALO - 1:POOL_HALO, :], g) * (i > 0).astype(F32)
    next_row = _rms(xn_ref[0:1, :], g) * (i < nt - 1).astype(F32)
    row = lax.broadcasted_iota(jnp.int32, (ts, 1), 0)
    prev = jnp.where(row == 0, prev_row, pltpu.roll(hn, 1, axis=0))
    nxt = jnp.where(row == ts - 1, next_row, pltpu.roll(hn, ts - 1, axis=0))
    xx = 0.5 * (prev + nxt) - hn

    def mixed(n):
        return (hn + xx * mix_ref[n:n + 1, :]).astype(BF16)

    xr_ref[...] = mixed(MIX_R)
    xk_ref[...] = mixed(MIX_K)
    xv_ref[...] = mixed(MIX_V)
    for n, mode, w_ref, h_ref in ((MIX_W, "decay", ww_ref, hw_ref), (MIX_A, "iclr", wa_ref, ha_ref),
                                  (MIX_G, "gate", wg_ref, hg_ref)):
        h_ref[...] = LORA_ACT[mode](_dot(mixed(n), w_ref[...])).astype(BF16)


def _lora_first_stage(w1):
    ne, dim, rank = w1.shape
    hid = -(-rank // LANES) * LANES
    w1p = jnp.pad(w1, ((0, 0), (0, 0), (0, hid - rank)))
    return jnp.transpose(w1p, (1, 0, 2)).reshape(dim, ne * hid).astype(BF16)


def _rwkv_prep(h, g, mix, w1_decay, w1_iclr, w1_gate):
    bsz, seq, dim = h.shape
    ts = min(seq, 512)
    nt = seq // ts
    hb = ts // POOL_HALO
    nhb = seq // POOL_HALO
    w1s = [_lora_first_stage(w) for w in (w1_decay, w1_iclr, w1_gate)]
    spec = pl.BlockSpec((None, ts, dim), lambda b, i: (b, i, 0))
    h_specs = [pl.BlockSpec((None, ts, w.shape[1]), lambda b, i: (b, i, 0)) for w in w1s]
    return pl.pallas_call(
        functools.partial(_prep_kernel, ts=ts),
        grid=(bsz, nt),
        in_specs=[
            spec,
            pl.BlockSpec((None, POOL_HALO, dim), lambda b, i: (b, jnp.maximum(i * hb - 1, 0), 0)),
            pl.BlockSpec((None, POOL_HALO, dim), lambda b, i: (b, jnp.minimum((i + 1) * hb, nhb - 1), 0)),
            pl.BlockSpec((1, dim), lambda b, i: (0, 0)),
            pl.BlockSpec(mix.shape, lambda b, i: (0, 0)),
        ] + [pl.BlockSpec(w.shape, lambda b, i: (0, 0)) for w in w1s],
        out_specs=[spec] * 3 + h_specs,
        out_shape=[jax.ShapeDtypeStruct(h.shape, BF16)] * 3
        + [jax.ShapeDtypeStruct((bsz, seq, w.shape[1]), BF16) for w in w1s],
        compiler_params=_cparams(("parallel", "arbitrary")),
        name="rwkv_prep",
    )(h, h, h, g.reshape(1, dim), mix, *w1s)


def _mm_kernel(x_ref, w_ref, o_ref):
    o_ref[...] = _dot(x_ref[...], w_ref[...])


def _mm_key_kernel(x_ref, w_ref, kk_scale_ref, k_ref, kk_ref):
    k = _dot(x_ref[...], w_ref[...])
    k_ref[...] = k
    ones_bd = _head_ones()
    for c in range(k.shape[1] // LANES):
        cs = slice(c * LANES, (c + 1) * LANES)
        kq = k[:, cs] * kk_scale_ref[:, cs]
        ss = _head_sum(kq * kq, ones_bd)
        kk_ref[:, cs] = kq * lax.rsqrt(jnp.maximum(ss, 1e-24))


def _proj(x2d, w, kk_scale=None):
    t, kdim = x2d.shape
    n = w.shape[1]
    tm = min(t, 1024)
    tn = min(n, 1024)
    x_spec = pl.BlockSpec((tm, kdim), lambda i, j: (i, 0))
    w_spec = pl.BlockSpec((kdim, tn), lambda i, j: (0, j))
    o_spec = pl.BlockSpec((tm, tn), lambda i, j: (i, j))
    o_shape = jax.ShapeDtypeStruct((t, n), F32)
    if kk_scale is None:
        return pl.pallas_call(
            _mm_kernel, grid=(t // tm, n // tn), in_specs=[x_spec, w_spec], out_specs=o_spec,
            out_shape=o_shape, compiler_params=_cparams(("parallel", "arbitrary")), name="rwkv_proj",
        )(x2d, w.astype(BF16))
    return pl.pallas_call(
        _mm_key_kernel, grid=(t // tm, n // tn),
        in_specs=[x_spec, w_spec, pl.BlockSpec((1, tn), lambda i, j: (0, j))],
        out_specs=[o_spec, o_spec], out_shape=[o_shape, o_shape],
        compiler_params=_cparams(("parallel", "arbitrary")), name="rwkv_proj_key",
    )(x2d, w.astype(BF16), kk_scale.reshape(1, n))


def _lora_kernel(h_ref, w2_ref, b_ref, o_ref, *, mode, hid):
    h = h_ref[...]
    for e in range(o_ref.shape[0]):
        z = _dot(h[:, e * hid:(e + 1) * hid], w2_ref[e])
        if mode == "decay":
            z = -jax.nn.sigmoid(b_ref[e] + z) * jnp.exp(F32(-0.5))
        elif mode == "iclr":
            z = jax.nn.sigmoid(b_ref[e] + z)
        o_ref[e] = z


def _lora(h2d, w2, bias, mode):
    t = h2d.shape[0]
    ne, rank, dim = w2.shape
    hid = h2d.shape[1] // ne
    w2p = jnp.pad(w2, ((0, 0), (0, hid - rank), (0, 0))).astype(BF16)
    tm = min(t, 512)
    return pl.pallas_call(
        functools.partial(_lora_kernel, mode=mode, hid=hid),
        grid=(t // tm,),
        in_specs=[
            pl.BlockSpec((tm, ne * hid), lambda i: (i, 0)),
            pl.BlockSpec((ne, hid, dim), lambda i: (0, 0, 0)),
            pl.BlockSpec((ne, 1, dim), lambda i: (0, 0, 0)),
        ],
        out_specs=pl.BlockSpec((ne, tm, dim), lambda i: (0, i, 0)),
        out_shape=jax.ShapeDtypeStruct((ne, t, dim), F32),
        compiler_params=_cparams(("parallel",)),
        name="rwkv_lora_" + mode,
    )(h2d, w2p, bias.reshape(ne, 1, dim))


def _pair_blockdiag(x):
    lane = lax.broadcasted_iota(jnp.int32, x.shape, 1)
    top = jnp.where(lane < HEAD_SIZE, x, 0.0)
    bot = jnp.where(lane >= HEAD_SIZE, x, 0.0)
    return jnp.concatenate([top, bot], axis=0).astype(BF16)


def _pair_dot(a, b):
    return _dot(a.astype(BF16), _pair_blockdiag(b))


def _wkv_kernel(r_ref, k_ref, v_ref, kk_ref, lw_ref, a_ref, ka_ref, *rest):
    n_cast = (len(rest) - 2) // 2
    y_ref, s_ref = rest[n_cast], rest[-1]
    for w_ref, wo_ref in zip(rest[:n_cast], rest[n_cast + 1:-1]):
        wo_ref[...] = w_ref[...].astype(BF16)
    d = pl.program_id(0)
    sgn = 1 - 2 * d
    fwd_f = (1 - d).astype(F32)
    L = CHUNK

    @pl.when(pl.program_id(2) == 0)
    def _():
        s_ref[...] = jnp.zeros_like(s_ref)

    t_sq = lax.broadcasted_iota(jnp.int32, (L, L), 0)
    s_sq = lax.broadcasted_iota(jnp.int32, (L, L), 1)
    tri = (((t_sq - s_sq) * sgn) >= 0).astype(BF16)
    t_p = lax.broadcasted_iota(jnp.int32, (L, LANES), 0)
    s_p = lax.broadcasted_iota(jnp.int32, (L, LANES), 1) % HEAD_SIZE
    rel = (t_p - s_p) * sgn
    m_strict = rel > 0
    m_read = (rel + (1 - d)) > 0
    eye = (t_p == s_p).astype(F32)
    row_h = lax.broadcasted_iota(jnp.int32, (LANES, LANES), 0) // HEAD_SIZE
    col_h = lax.broadcasted_iota(jnp.int32, (LANES, LANES), 1) // HEAD_SIZE
    same_head = row_h == col_h
    n_pairs = s_ref.shape[0]
    n_sub = r_ref.shape[0] // L
    sub_rows = [pl.ds(pl.multiple_of((i + d * (n_sub - 1 - 2 * i)) * L, L), L) for i in range(n_sub)]
    cols = [slice(p * LANES, (p + 1) * LANES) for p in range(n_pairs)]
    rows = [rw for rw in sub_rows for _ in cols]
    ccol = [cs for _ in sub_rows for cs in cols]

    def each(fn, *lists):
        return [fn(*args) for args in zip(*lists)]

    def load(ref):
        return [ref[rw, cs] for rw, cs in zip(rows, ccol)]

    def cumulative(lw):
        m = _dot(tri, jnp.concatenate(_split3(lw), axis=1))
        return m[:, :LANES] + m[:, LANES:2 * LANES] + m[:, 2 * LANES:]

    def halves(m):
        return m[:, :LANES], m[:, LANES:]

    def lanes2(x, y):
        return jnp.concatenate([_pair_blockdiag(x), _pair_blockdiag(y)], axis=1)

    lw = load(lw_ref)
    c_incl = each(cumulative, lw)
    c_excl = each(lambda c, w: c - w, c_incl, lw)
    tot = each(lambda w: jnp.sum(w, axis=0, keepdims=True), lw)
    kk = load(kk_ref)
    a = load(a_ref)
    beta = each(lambda x, y: x * y, kk, a)
    kt = each(lambda x, a_p, cs: x * (1.0 + (a_p - 1.0) * ka_ref[:, cs]), load(k_ref), a, ccol)
    ah = each(lambda x, c: -x * jnp.exp(c), kk, c_excl)
    rh = each(lambda x, c, w: x * jnp.exp(c + fwd_f * w), load(r_ref), c_excl, lw)
    e_neg = each(lambda c: jnp.exp(-c), c_incl)
    lhs = each(lambda x, y: jnp.concatenate([x, y], axis=0).astype(BF16), ah, rh)
    a_bk = each(lambda l, b, k_, e: _dot_nt(l, jnp.concatenate([_pair_blockdiag(b * e), _pair_blockdiag(k_ * e)],
                                                               axis=0)), lhs, beta, kt, e_neg)
    n_ab = each(lambda m: jnp.where(m_strict, m[:L, :LANES], 0.0), a_bk)
    n_ak = each(lambda m: jnp.where(m_strict, m[:L, LANES:], 0.0), a_bk)
    n_rb = each(lambda m: jnp.where(m_read, m[L:, :LANES], 0.0), a_bk)
    n_rk = each(lambda m: jnp.where(m_read, m[L:, LANES:], 0.0), a_bk)
    v = load(v_ref)
    xy = each(lambda x, y, v_p: _dot(jnp.concatenate([x, y], axis=0).astype(BF16), _pair_blockdiag(v_p)),
              n_ak, n_rk, v)
    tm = each(lambda n: eye + n, n_ab)
    pw = each(_pair_dot, n_ab, n_ab)
    for _ in range(1, L.bit_length() - 2):
        st = each(lambda q, t: _dot(jnp.concatenate([q, t], axis=0).astype(BF16), _pair_blockdiag(q)), pw, tm)
        tm = each(lambda t, m: t + m[L:], tm, st)
        pw = each(lambda m: m[:L], st)
    tm = each(lambda t, q: t + _pair_dot(t, q), tm, pw)
    at_ui = each(lambda t, x, m: halves(_dot(t.astype(BF16), lanes2(x, m[:L]))), tm, ah, xy)
    rt_yi = each(lambda n, au: halves(_dot(n.astype(BF16), lanes2(au[0], au[1]))), n_rb, at_ui)
    rt = each(lambda x, ry: x + ry[0], rh, rt_yi)
    yi = each(lambda ry, m: ry[1] + m[L:], rt_yi, xy)
    e_bar = each(lambda t, c: jnp.exp(t - c), tot, c_incl)
    rhs = each(lambda b, k_, e: jnp.concatenate([b * e, k_ * e], axis=0).astype(BF16), beta, kt, e_bar)
    for i, rw in enumerate(sub_rows):
        ch = slice(i * n_pairs, (i + 1) * n_pairs)
        s = [s_ref[p] for p in range(n_pairs)]
        uy = each(lambda au, x, s_p: _dot_nt(jnp.concatenate([au[0], x], axis=0).astype(BF16), s_p.astype(BF16)),
                  at_ui[ch], rt[ch], s)
        for cs, uy_p, yi_p in zip(cols, uy, yi[ch]):
            y_ref[rw, cs] = uy_p[L:] + yi_p
        uv_t = each(lambda uy_p, au, v_p: jnp.concatenate([uy_p[:L] + au[1], v_p], axis=0).T.astype(BF16),
                    uy, at_ui[ch], v[ch])
        upd = each(_dot, uv_t, rhs[ch])
        for p in range(n_pairs):
            s_ref[p] = s[p] * jnp.exp(tot[ch][p]) + jnp.where(same_head, upd[p], 0.0)


BF16_SUBLANES = 16
WKV_CHUNKS_PER_STEP = 2


def _wkv(r, k, v, kk, lw, a, k_a, cast_weights=()):
    bsz, seq, dim = r.shape
    sub = WKV_CHUNKS_PER_STEP if (seq // CHUNK) % WKV_CHUNKS_PER_STEP == 0 else 1
    rows = sub * CHUNK
    nc = seq // rows
    steps = 2 * bsz * nc

    def cidx(d, c):
        return c + d * (nc - 1 - 2 * c)

    def step(d, b, c):
        return (d * bsz + b) * nc + c

    flat = [w.reshape(-1, w.shape[-1]) for w in cast_weights]
    slab = [w.shape[0] // steps for w in flat]
    in_kernel = all(w.shape[0] % steps == 0 and s % BF16_SUBLANES == 0 for w, s in zip(flat, slab))
    if not in_kernel:
        flat, slab = [], []
    w_specs = [pl.BlockSpec((s, w.shape[1]), lambda d, b, c: (step(d, b, c), 0)) for w, s in zip(flat, slab)]
    tok = pl.BlockSpec((None, rows, dim), lambda d, b, c: (b, cidx(d, c), 0))
    dirtok = pl.BlockSpec((None, None, rows, dim), lambda d, b, c: (d, b, cidx(d, c), 0))
    outs = pl.pallas_call(
        _wkv_kernel,
        grid=(2, bsz, nc),
        in_specs=[tok, tok, tok, tok, dirtok, dirtok, pl.BlockSpec((1, dim), lambda d, b, c: (0, 0))] + w_specs,
        out_specs=[dirtok] + w_specs,
        out_shape=[jax.ShapeDtypeStruct((2, bsz, seq, dim), F32)]
        + [jax.ShapeDtypeStruct(w.shape, BF16) for w in flat],
        scratch_shapes=[pltpu.VMEM((dim // LANES, LANES, LANES), F32)],
        compiler_params=_cparams(("arbitrary", "arbitrary", "arbitrary")),
        name="wkv_scan",
    )(r, k, v, kk, lw, a, k_a.reshape(1, dim), *flat)
    if in_kernel:
        return outs[0], [o.reshape(w.shape) for o, w in zip(outs[1:], cast_weights)]
    return outs[0], [w.astype(BF16) for w in cast_weights]


def _rwkv_out_kernel(yf_ref, yb_ref, r_ref, k_ref, v_ref, a_ref, g_ref, h_ref,
                     ka_ref, rk_ref, lw_ref, lb_ref, wo_ref, o_ref, z_ref):
    ones_bd = _head_ones()
    inv_n = 1.0 / HEAD_SIZE
    for c in range(h_ref.shape[1] // LANES):
        cs = slice(c * LANES, (c + 1) * LANES)
        y = yf_ref[:, cs] + yb_ref[:, cs]
        mu = _head_sum(y, ones_bd) * inv_n
        yc = y - mu
        var = _head_sum(yc * yc, ones_bd) * inv_n
        yn = yc * lax.rsqrt(var + LNX_EPS) * lw_ref[:, cs] + lb_ref[:, cs]
        kd = k_ref[:, cs] * (1.0 + (a_ref[:, cs] - 1.0) * ka_ref[:, cs])
        bonus = _head_sum(r_ref[:, cs] * kd * rk_ref[:, cs], ones_bd) * v_ref[:, cs]
        z_ref[:, cs] = ((yn + bonus) * g_ref[:, cs]).astype(BF16)
    o_ref[...] = h_ref[...] + _dot(z_ref[...], wo_ref[...])


def _rwkv_out(y, r, k, v, a, g, h2d, k_a, r_k, lnx_w, lnx_b, w_o):
    t, dim = h2d.shape
    tm = min(t, 256)
    tok = pl.BlockSpec((tm, dim), lambda i: (i, 0))
    vec = pl.BlockSpec((1, dim), lambda i: (0, 0))
    return pl.pallas_call(
        _rwkv_out_kernel,
        grid=(t // tm,),
        in_specs=[pl.BlockSpec((None, tm, dim), lambda i: (0, i, 0)),
                  pl.BlockSpec((None, tm, dim), lambda i: (1, i, 0)),
                  tok, tok, tok,
                  pl.BlockSpec((None, tm, dim), lambda i: (0, i, 0)),
                  tok, tok, vec, vec, vec, vec,
                  pl.BlockSpec((dim, dim), lambda i: (0, 0))],
        out_specs=tok,
        out_shape=jax.ShapeDtypeStruct((t, dim), F32),
        scratch_shapes=[pltpu.VMEM((tm, dim), BF16)],
        compiler_params=_cparams(("parallel",)),
        name="rwkv_out",
    )(y, y, r, k, v, a, g, h2d, k_a.reshape(1, dim), r_k.reshape(1, dim),
      lnx_w.reshape(1, dim), lnx_b.reshape(1, dim), w_o.astype(BF16))


ROUTE_E1, ROUTE_E2, ROUTE_G1, ROUTE_G2, ROUTE_R1, ROUTE_R2 = range(6)


def _router_kernel(x_ref, g_ref, w_ref, hn_ref, route_ref, cnt_ref, *, n_exp, tm):
    i = pl.program_id(0)

    @pl.when(i == 0)
    def _():
        cnt_ref[...] = jnp.zeros_like(cnt_ref)

    hn = _rms(x_ref[...], g_ref[...])
    for c in range(hn_ref.shape[1]):
        hn_ref[:, c, :] = hn[:, c * LANES:(c + 1) * LANES]
    x_hi, x_mid, x_lo = _split3(hn)
    w_hi, w_mid, w_lo = w_ref[0], w_ref[1], w_ref[2]
    logits = (_dot(x_hi, w_hi) + _dot(x_hi, w_mid) + _dot(x_mid, w_hi)
              + _dot(x_hi, w_lo) + _dot(x_mid, w_mid) + _dot(x_lo, w_hi))
    lane = lax.broadcasted_iota(jnp.int32, (tm, LANES), 1)
    neg = F32(-jnp.inf)
    lg = jnp.where(lane < n_exp, logits, neg)
    m1 = jnp.max(lg, axis=-1, keepdims=True)
    i1 = jnp.min(jnp.where(lg == m1, lane, LANES), axis=-1, keepdims=True)
    lg2 = jnp.where(lane == i1, neg, lg)
    m2 = jnp.max(lg2, axis=-1, keepdims=True)
    i2 = jnp.min(jnp.where(lg2 == m2, lane, LANES), axis=-1, keepdims=True)
    e2 = jnp.exp(m2 - m1)
    den = 1.0 + e2
    g1 = 1.0 / den
    g2 = e2 / den
    oh1 = lane == i1
    oh2 = lane == i2
    oh = oh1.astype(F32) + oh2.astype(F32)
    rr = lax.broadcasted_iota(jnp.int32, (tm, tm), 0)
    cc = lax.broadcasted_iota(jnp.int32, (tm, tm), 1)
    before = (cc < rr).astype(BF16)
    prefix = _dot(before, oh.astype(BF16)) + cnt_ref[...]
    r1 = jnp.sum(jnp.where(oh1, prefix, 0.0), axis=-1, keepdims=True)
    r2 = jnp.sum(jnp.where(oh2, prefix, 0.0), axis=-1, keepdims=True)
    cnt_ref[...] += jnp.sum(oh, axis=0, keepdims=True)
    route = jnp.zeros((tm, LANES), F32)
    for col, val in ((ROUTE_E1, i1.astype(F32)), (ROUTE_E2, i2.astype(F32)), (ROUTE_G1, g1),
                     (ROUTE_G2, g2), (ROUTE_R1, r1), (ROUTE_R2, r2)):
        route = jnp.where(lane == col, val, route)
    route_ref[...] = route


def _router(h2d, g, router_w):
    t, dim = h2d.shape
    n_exp = router_w.shape[1]
    tm = min(t, 512)
    wp = jnp.pad(router_w, ((0, 0), (0, LANES - n_exp)))
    w3 = jnp.stack(_split3(wp))
    return pl.pallas_call(
        functools.partial(_router_kernel, n_exp=n_exp, tm=tm),
        grid=(t // tm,),
        in_specs=[pl.BlockSpec((tm, dim), lambda i: (i, 0)),
                  pl.BlockSpec((1, dim), lambda i: (0, 0)),
                  pl.BlockSpec((3, dim, LANES), lambda i: (0, 0, 0))],
        out_specs=[pl.BlockSpec((tm, dim // LANES, LANES), lambda i: (i, 0, 0)),
                   pl.BlockSpec((tm, LANES), lambda i: (i, 0)),
                   pl.BlockSpec((1, LANES), lambda i: (0, 0))],
        out_shape=[jax.ShapeDtypeStruct((t, dim // LANES, LANES), F32),
                   jax.ShapeDtypeStruct((t, LANES), F32),
                   jax.ShapeDtypeStruct((1, LANES), F32)],
        compiler_params=_cparams(("arbitrary",)),
        name="moe_router",
    )(h2d, g.reshape(1, dim), w3)


def _row_gather(src_hbm, dst_vmem, sem, index_of, rows):
    def copy(t):
        return pltpu.make_async_copy(src_hbm.at[pl.ds(index_of(t), 1)], dst_vmem.at[pl.ds(t, 1)], sem)

    def start():
        lax.fori_loop(0, rows, lambda t, c: (copy(t).start(), c)[1], 0, unroll=8)

    def wait():
        for t in range(rows):
            pltpu.make_async_copy(src_hbm.at[pl.ds(0, 1)], dst_vmem.at[pl.ds(t, 1)], sem).wait()

    return start, wait


def _dispatch_kernel(src_ref, hn_ref, xs_ref, buf_ref, sem, *, rows):
    i = pl.program_id(0)
    n = pl.num_programs(0)

    def gather(step):
        slot = step % 2
        return _row_gather(hn_ref, buf_ref.at[slot], sem.at[slot], lambda t: src_ref[step * rows + t], rows)

    @pl.when(i == 0)
    def _():
        gather(i)[0]()

    @pl.when(i + 1 < n)
    def _():
        gather(i + 1)[0]()

    gather(i)[1]()
    slot = i % 2
    for c in range(buf_ref.shape[2]):
        xs_ref[:, c * LANES:(c + 1) * LANES] = buf_ref[slot, :, c, :].astype(BF16)


def _dispatch(hn, src_rows):
    tile_shape = hn.shape[1:]
    dim = tile_shape[0] * tile_shape[1]
    n_rows = src_rows.shape[0]
    rows = min(n_rows, 512)
    return pl.pallas_call(
        functools.partial(_dispatch_kernel, rows=rows),
        grid_spec=pltpu.PrefetchScalarGridSpec(
            num_scalar_prefetch=1,
            grid=(n_rows // rows,),
            in_specs=[pl.BlockSpec(memory_space=pl.ANY)],
            out_specs=pl.BlockSpec((rows, dim), lambda i, s: (i, 0)),
            scratch_shapes=[pltpu.VMEM((2, rows) + tile_shape, F32), pltpu.SemaphoreType.DMA((2,))],
        ),
        out_shape=jax.ShapeDtypeStruct((n_rows, dim), BF16),
        compiler_params=_cparams(("arbitrary",)),
        name="moe_dispatch",
    )(src_rows, hn)


def _expert_kernel(te_ref, nu_ref, x_ref, wg_ref, wu_ref, wd_ref, o_ref, acc_ref):
    del te_ref
    i = pl.program_id(0)
    j = pl.program_id(1)
    used = i < nu_ref[0]

    @pl.when(j == 0)
    def _():
        acc_ref[...] = jnp.zeros_like(acc_ref)

    @pl.when(used)
    def _():
        x = x_ref[...]
        gate = _dot(x, wg_ref[...])
        up = _dot(x, wu_ref[...])
        act = (gate * jax.nn.sigmoid(gate) * up).astype(BF16)
        acc_ref[...] += _dot(act, wd_ref[...])

    @pl.when(j == pl.num_programs(1) - 1)
    def _():
        for c in range(o_ref.shape[1]):
            o_ref[:, c, :] = acc_ref[:, c * LANES:(c + 1) * LANES]


def _expert_tiles(ff):
    for tf in (1024, 512, 256, 128):
        if ff % tf == 0:
            return tf
    return ff


def _experts(xs, tile_expert, n_used, w_gate, w_up, w_down, tm):
    n_rows, dim = xs.shape
    ff = w_gate.shape[2]
    tf = _expert_tiles(ff)
    nj = ff // tf

    def jj(i, j, nu):
        return jnp.where(i < nu[0], j, nj - 1)

    return pl.pallas_call(
        _expert_kernel,
        grid_spec=pltpu.PrefetchScalarGridSpec(
            num_scalar_prefetch=2,
            grid=(n_rows // tm, nj),
            in_specs=[pl.BlockSpec((tm, dim), lambda i, j, te, nu: (i, 0)),
                      pl.BlockSpec((None, dim, tf), lambda i, j, te, nu: (te[i], 0, jj(i, j, nu))),
                      pl.BlockSpec((None, dim, tf), lambda i, j, te, nu: (te[i], 0, jj(i, j, nu))),
                      pl.BlockSpec((None, tf, dim), lambda i, j, te, nu: (te[i], jj(i, j, nu), 0))],
            out_specs=pl.BlockSpec((tm, dim // LANES, LANES), lambda i, j, te, nu: (i, 0, 0)),
            scratch_shapes=[pltpu.VMEM((tm, dim), F32)],
        ),
        out_shape=jax.ShapeDtypeStruct((n_rows, dim // LANES, LANES), F32),
        compiler_params=_cparams(("arbitrary", "arbitrary")),
        name="moe_experts",
    )(tile_expert, n_used, xs, w_gate.astype(BF16), w_up.astype(BF16), w_down.astype(BF16))


def _combine_kernel(slots_ref, h_ref, route_ref, g_ref, y_ref, o_ref, buf_ref, sem, *, rows):
    i = pl.program_id(0)
    n = pl.num_programs(0)

    def gather(step, k):
        slot = step % 2
        return _row_gather(y_ref, buf_ref.at[slot, k], sem.at[slot],
                           lambda t: slots_ref[TOP_K * (step * rows + t) + k], rows)

    @pl.when(i == 0)
    def _():
        for k in range(TOP_K):
            gather(i, k)[0]()

    @pl.when(i + 1 < n)
    def _():
        for k in range(TOP_K):
            gather(i + 1, k)[0]()

    for k in range(TOP_K):
        gather(i, k)[1]()
    slot = i % 2
    route = route_ref[...]
    g1 = route[:, ROUTE_G1:ROUTE_G1 + 1]
    g2 = route[:, ROUTE_G2:ROUTE_G2 + 1]
    dim = h_ref.shape[1]
    ss = jnp.zeros((rows, 1), F32)
    for c in range(dim // LANES):
        cs = slice(c * LANES, (c + 1) * LANES)
        hc = h_ref[:, cs] + g1 * buf_ref[slot, 0, :, c, :] + g2 * buf_ref[slot, 1, :, c, :]
        o_ref[:, cs] = hc
        ss = ss + jnp.sum(hc * hc, axis=-1, keepdims=True)
    o_ref[...] = o_ref[...] * lax.rsqrt(ss * (1.0 / dim) + RMS_EPS) * g_ref[...]


def _combine(h2d, route, slots, y, g):
    t, dim = h2d.shape
    tile_shape = y.shape[1:]
    rows = min(t, 256)
    return pl.pallas_call(
        functools.partial(_combine_kernel, rows=rows),
        grid_spec=pltpu.PrefetchScalarGridSpec(
            num_scalar_prefetch=1,
            grid=(t // rows,),
            in_specs=[pl.BlockSpec((rows, dim), lambda i, s: (i, 0)),
                      pl.BlockSpec((rows, LANES), lambda i, s: (i, 0)),
                      pl.BlockSpec((1, dim), lambda i, s: (0, 0)),
                      pl.BlockSpec(memory_space=pl.ANY)],
            out_specs=pl.BlockSpec((rows, dim), lambda i, s: (i, 0)),
            scratch_shapes=[pltpu.VMEM((2, TOP_K, rows) + tile_shape, F32), pltpu.SemaphoreType.DMA((2,))],
        ),
        out_shape=jax.ShapeDtypeStruct((t, dim), F32),
        compiler_params=_cparams(("arbitrary",)),
        name="moe_combine",
    )(slots, h2d, route, g.reshape(1, dim), y)


def _moe_layer(h2d, norm_g, router_w, w_gate, w_up, w_down, final_g):
    t, dim = h2d.shape
    n_exp = router_w.shape[1]
    tm = min(t, 512)
    hn, route, counts = _router(h2d, norm_g, router_w)
    counts = counts[0, :n_exp].astype(jnp.int32)
    tiles = (counts + tm - 1) // tm
    tile_end = jnp.cumsum(tiles)
    row_start = (tile_end - tiles) * tm
    e12 = route[:, ROUTE_E1:ROUTE_E2 + 1].astype(jnp.int32)
    r12 = route[:, ROUTE_R1:ROUTE_R2 + 1].astype(jnp.int32)
    slots = (row_start[e12] + r12).reshape(-1)
    n_tiles = TOP_K * t // tm + n_exp
    n_used = tile_end[-1:]
    tile_ids = jnp.minimum(jnp.arange(n_tiles, dtype=jnp.int32), n_used[0] - 1)
    tile_expert = jnp.sum(tile_ids[:, None] >= tile_end[None, :], axis=1).astype(jnp.int32)
    token = jnp.repeat(jnp.arange(t, dtype=jnp.int32), TOP_K)
    src_rows = jnp.zeros((n_tiles * tm,), jnp.int32).at[slots].set(token)
    xs = _dispatch(hn, src_rows)
    y = _experts(xs, tile_expert, n_used, w_gate, w_up, w_down, tm)
    return _combine(h2d, route, slots, y, final_g)


def kernel(x, l0_norm_mix, l0_pool_w, l0_pool_scale, l0_norm_ffn, l0_ffn_w_gate, l0_ffn_w_up, l0_ffn_w_down, l1_norm_mix, l1_mix, l1_w_r, l1_w_k, l1_w_v, l1_w_o, l1_decay_w0, l1_decay_w1, l1_decay_w2, l1_iclr_a0, l1_iclr_a1, l1_iclr_a2, l1_gate_g1, l1_gate_g2, l1_k_k, l1_k_a, l1_r_k, l1_lnx_w, l1_lnx_b, l1_norm_ffn, l1_router, l1_moe_w_gate, l1_moe_w_up, l1_moe_w_down, final_norm):
    bsz, seq, dim = x.shape
    t = bsz * seq
    h = _pool_layer(x, l0_norm_mix, l0_pool_w, l0_pool_scale)
    h = _ffn_layer(h.reshape(t, dim), l0_norm_ffn, l0_ffn_w_gate, l0_ffn_w_up, l0_ffn_w_down)
    xr, xk, xv, hw, ha, hg = [m.reshape(t, m.shape[-1]) for m in _rwkv_prep(
        h.reshape(bsz, seq, dim), l1_norm_mix, l1_mix, l1_decay_w1, l1_iclr_a1, l1_gate_g1[None])]
    r = _proj(xr, l1_w_r)
    k, kk = _proj(xk, l1_w_k, kk_scale=l1_k_k)
    v = _proj(xv, l1_w_v)
    lw = _lora(hw, l1_decay_w2, l1_decay_w0, "decay")
    a = _lora(ha, l1_iclr_a2, l1_iclr_a0, "iclr")
    g = _lora(hg, l1_gate_g2[None], jnp.zeros((1, dim), F32), "gate")[0]
    b3 = lambda z: z.reshape(bsz, seq, dim)
    b4 = lambda z: z.reshape(2, bsz, seq, dim)
    y, (moe_wg, moe_wu, moe_wd) = _wkv(b3(r), b3(k), b3(v), b3(kk), b4(lw), b4(a), l1_k_a,
                                       cast_weights=(l1_moe_w_gate, l1_moe_w_up, l1_moe_w_down))
    h = _rwkv_out(y.reshape(2, t, dim), r, k, v, a, g, h, l1_k_a, l1_r_k, l1_lnx_w, l1_lnx_b, l1_w_o)
    out = _moe_layer(h, l1_norm_ffn, l1_router, moe_wg, moe_wu, moe_wd, final_norm)
    return out.reshape(bsz, seq, dim)
```

```python
import functools

import jax
import jax.numpy as jnp
from jax import lax
from jax.experimental import pallas as pl
from jax.experimental.pallas import tpu as pltpu

F32 = jnp.float32
BF16 = jnp.bfloat16

HEAD_SIZE = 64
LANES = 128
POOL_WINDOWS = (2, 4, 8, 16)
POOL_HALO = 8
TOP_K = 2
RMS_EPS = 1e-6
LNX_EPS = 64e-5
CHUNK = 64
VMEM_LIMIT_BYTES = 56 * 1024 * 1024


def _cparams(sem):
    return pltpu.CompilerParams(dimension_semantics=sem, vmem_limit_bytes=VMEM_LIMIT_BYTES)


def _dot(a, b):
    return jnp.dot(a, b, preferred_element_type=F32)


def _dot_nt(a, b):
    return lax.dot_general(a, b, (((1,), (1,)), ((), ())), preferred_element_type=F32)


def _rms(x, g):
    return x * lax.rsqrt(jnp.mean(x * x, axis=-1, keepdims=True) + RMS_EPS) * g


def _split3(x):
    hi = x.astype(BF16)
    r1 = x - hi.astype(F32)
    mid = r1.astype(BF16)
    lo = (r1 - mid.astype(F32)).astype(BF16)
    return hi, mid, lo


HEAD_SUM_LANES = 256


def _head_ones():
    r = lax.broadcasted_iota(jnp.int32, (HEAD_SUM_LANES, HEAD_SUM_LANES), 0) // HEAD_SIZE
    c = lax.broadcasted_iota(jnp.int32, (HEAD_SUM_LANES, HEAD_SUM_LANES), 1) // HEAD_SIZE
    return (r == c).astype(BF16)


def _head_sum(x, ones_bd):
    hi, mid, _ = _split3(x)
    return _dot(hi, ones_bd) + _dot(mid, ones_bd)


def _pool_kernel(x_ref, xp_ref, xn_ref, g_ref, pw_ref, ps_ref, o_ref, ext_ref, *, seq, ts, cg):
    i = pl.program_id(1)
    nt = pl.num_programs(1)
    g = g_ref[...]
    x = x_ref[...]
    hn = _rms(x, g)
    has_prev = (i > 0).astype(F32)
    has_next = (i < nt - 1).astype(F32)
    ext_ref[0:POOL_HALO, :] = _rms(xp_ref[...], g) * has_prev
    ext_ref[POOL_HALO:POOL_HALO + ts, :] = hn
    ext_ref[POOL_HALO + ts:POOL_HALO + ts + POOL_HALO, :] = _rms(xn_ref[...], g) * has_next
    pos = i * ts + lax.broadcasted_iota(jnp.int32, (ts, 1), 0)
    for gi, w in enumerate(POOL_WINDOWS):
        cs = slice(gi * cg, (gi + 1) * cg)
        lo_off = -(w // 2)
        hi_off = w - w // 2 - 1
        tot = None
        for o in range(lo_off, hi_off + 1):
            part = ext_ref[POOL_HALO + o:POOL_HALO + o + ts, cs]
            tot = part if tot is None else tot + part
        lo = jnp.maximum(pos + lo_off, 0)
        hi = jnp.minimum(pos + hi_off + 1, seq)
        cnt = (hi - lo).astype(F32)
        pooled = tot / cnt - hn[:, cs]
        mixed = _dot(pooled.astype(BF16), pw_ref[gi])
        o_ref[:, cs] = x[:, cs] + mixed * ps_ref[:, cs]


def _pool_layer(x, g, pool_w, pool_scale):
    bsz, seq, dim = x.shape
    ng = len(POOL_WINDOWS)
    cg = dim // ng
    ts = min(seq, 512)
    nt = seq // ts
    hb = ts // POOL_HALO
    nhb = seq // POOL_HALO
    kern = functools.partial(_pool_kernel, seq=seq, ts=ts, cg=cg)
    return pl.pallas_call(
        kern,
        grid=(bsz, nt),
        in_specs=[
            pl.BlockSpec((None, ts, dim), lambda b, i: (b, i, 0)),
            pl.BlockSpec((None, POOL_HALO, dim), lambda b, i: (b, jnp.maximum(i * hb - 1, 0), 0)),
            pl.BlockSpec((None, POOL_HALO, dim), lambda b, i: (b, jnp.minimum((i + 1) * hb, nhb - 1), 0)),
            pl.BlockSpec((1, dim), lambda b, i: (0, 0)),
            pl.BlockSpec((ng, cg, cg), lambda b, i: (0, 0, 0)),
            pl.BlockSpec((1, dim), lambda b, i: (0, 0)),
        ],
        out_specs=pl.BlockSpec((None, ts, dim), lambda b, i: (b, i, 0)),
        out_shape=jax.ShapeDtypeStruct(x.shape, F32),
        scratch_shapes=[pltpu.VMEM((ts + 2 * POOL_HALO, dim), F32)],
        compiler_params=_cparams(("parallel", "arbitrary")),
        name="pool_layer",
    )(x, x, x, g.reshape(1, dim), pool_w.astype(BF16), pool_scale.reshape(1, dim))


def _ffn_kernel(x_ref, g_ref, wg_ref, wu_ref, wd_ref, o_ref, hn_ref):
    j = pl.program_id(1)

    @pl.when(j == 0)
    def _():
        x = x_ref[...]
        hn_ref[...] = _rms(x, g_ref[...]).astype(BF16)
        o_ref[...] = x

    hn = hn_ref[...]
    gate = _dot(hn, wg_ref[...])
    up = _dot(hn, wu_ref[...])
    act = (gate * jax.nn.sigmoid(gate) * up).astype(BF16)
    o_ref[...] += _dot(act, wd_ref[...])


def _ffn_layer(x2d, g, w_gate, w_up, w_down):
    t, dim = x2d.shape
    ff = w_gate.shape[1]
    tm = min(t, 512)
    tf = 512 if ff % 512 == 0 else ff
    return pl.pallas_call(
        _ffn_kernel,
        grid=(t // tm, ff // tf),
        in_specs=[
            pl.BlockSpec((tm, dim), lambda i, j: (i, 0)),
            pl.BlockSpec((1, dim), lambda i, j: (0, 0)),
            pl.BlockSpec((dim, tf), lambda i, j: (0, j)),
            pl.BlockSpec((dim, tf), lambda i, j: (0, j)),
            pl.BlockSpec((tf, dim), lambda i, j: (j, 0)),
        ],
        out_specs=pl.BlockSpec((tm, dim), lambda i, j: (i, 0)),
        out_shape=jax.ShapeDtypeStruct((t, dim), F32),
        scratch_shapes=[pltpu.VMEM((tm, dim), BF16)],
        compiler_params=_cparams(("parallel", "arbitrary")),
        name="ffn_dense",
    )(x2d, g.reshape(1, dim), w_gate.astype(BF16), w_up.astype(BF16), w_down.astype(BF16))


MIX_R, MIX_W, MIX_K, MIX_V, MIX_A, MIX_G = range(6)
LORA_ACT = {"decay": jnp.tanh, "iclr": lambda z: z, "gate": jax.nn.sigmoid}


def _prep_kernel(x_ref, xp_ref, xn_ref, g_ref, mix_ref, ww_ref, wa_ref, wg_ref,
                 xr_ref, xk_ref, xv_ref, hw_ref, ha_ref, hg_ref, *, ts):
    i = pl.program_id(1)
    nt = pl.num_programs(1)
    g = g_ref[...]
    hn = _rms(x_ref[...], g)
    prev_row = _rms(xp_ref[POOL_HALO - 1:POOL_HALO, :], g) * (i > 0).astype(F32)
    next_row = _rms(xn_ref[0:1, :], g) * (i < nt - 1).astype(F32)
    row = lax.broadcasted_iota(jnp.int32, (ts, 1), 0)
    prev = jnp.where(row == 0, prev_row, pltpu.roll(hn, 1, axis=0))
    nxt = jnp.where(row == ts - 1, next_row, pltpu.roll(hn, ts - 1, axis=0))
    xx = 0.5 * (prev + nxt) - hn

    def mixed(n):
        return (hn + xx * mix_ref[n:n + 1, :]).astype(BF16)

    xr_ref[...] = mixed(MIX_R)
    xk_ref[...] = mixed(MIX_K)
    xv_ref[...] = mixed(MIX_V)
    for n, mode, w_ref, h_ref in ((MIX_W, "decay", ww_ref, hw_ref), (MIX_A, "iclr", wa_ref, ha_ref),
                                  (MIX_G, "gate", wg_ref, hg_ref)):
        h_ref[...] = LORA_ACT[mode](_dot(mixed(n), w_ref[...])).astype(BF16)


def _lora_first_stage(w1):
    ne, dim, rank = w1.shape
    hid = -(-rank // LANES) * LANES
    w1p = jnp.pad(w1, ((0, 0), (0, 0), (0, hid - rank)))
    return jnp.transpose(w1p, (1, 0, 2)).reshape(dim, ne * hid).astype(BF16)


def _rwkv_prep(h, g, mix, w1_decay, w1_iclr, w1_gate):
    bsz, seq, dim = h.shape
    ts = min(seq, 512)
    nt = seq // ts
    hb = ts // POOL_HALO
    nhb = seq // POOL_HALO
    w1s = [_lora_first_stage(w) for w in (w1_decay, w1_iclr, w1_gate)]
    spec = pl.BlockSpec((None, ts, dim), lambda b, i: (b, i, 0))
    h_specs = [pl.BlockSpec((None, ts, w.shape[1]), lambda b, i: (b, i, 0)) for w in w1s]
    return pl.pallas_call(
        functools.partial(_prep_kernel, ts=ts),
        grid=(bsz, nt),
        in_specs=[
            spec,
            pl.BlockSpec((None, POOL_HALO, dim), lambda b, i: (b, jnp.maximum(i * hb - 1, 0), 0)),
            pl.BlockSpec((None, POOL_HALO, dim), lambda b, i: (b, jnp.minimum((i + 1) * hb, nhb - 1), 0)),
            pl.BlockSpec((1, dim), lambda b, i: (0, 0)),
            pl.BlockSpec(mix.shape, lambda b, i: (0, 0)),
        ] + [pl.BlockSpec(w.shape, lambda b, i: (0, 0)) for w in w1s],
        out_specs=[spec] * 3 + h_specs,
        out_shape=[jax.ShapeDtypeStruct(h.shape, BF16)] * 3
        + [jax.ShapeDtypeStruct((bsz, seq, w.shape[1]), BF16) for w in w1s],
        compiler_params=_cparams(("parallel", "arbitrary")),
        name="rwkv_prep",
    )(h, h, h, g.reshape(1, dim), mix, *w1s)


def _mm_kernel(x_ref, w_ref, o_ref):
    o_ref[...] = _dot(x_ref[...], w_ref[...])


def _mm_key_kernel(x_ref, w_ref, kk_scale_ref, k_ref, kk_ref):
    k = _dot(x_ref[...], w_ref[...])
    k_ref[...] = k
    ones_bd = _head_ones()
    for c in range(k.shape[1] // HEAD_SUM_LANES):
        cs = slice(c * HEAD_SUM_LANES, (c + 1) * HEAD_SUM_LANES)
        kq = k[:, cs] * kk_scale_ref[:, cs]
        ss = _head_sum(kq * kq, ones_bd)
        kk_ref[:, cs] = kq * lax.rsqrt(jnp.maximum(ss, 1e-24))


def _proj(x2d, w, kk_scale=None):
    t, kdim = x2d.shape
    n = w.shape[1]
    tm = min(t, 1024)
    tn = min(n, 1024)
    x_spec = pl.BlockSpec((tm, kdim), lambda i, j: (i, 0))
    w_spec = pl.BlockSpec((kdim, tn), lambda i, j: (0, j))
    o_spec = pl.BlockSpec((tm, tn), lambda i, j: (i, j))
    o_shape = jax.ShapeDtypeStruct((t, n), F32)
    if kk_scale is None:
        return pl.pallas_call(
            _mm_kernel, grid=(t // tm, n // tn), in_specs=[x_spec, w_spec], out_specs=o_spec,
            out_shape=o_shape, compiler_params=_cparams(("parallel", "arbitrary")), name="rwkv_proj",
        )(x2d, w.astype(BF16))
    return pl.pallas_call(
        _mm_key_kernel, grid=(t // tm, n // tn),
        in_specs=[x_spec, w_spec, pl.BlockSpec((1, tn), lambda i, j: (0, j))],
        out_specs=[o_spec, o_spec], out_shape=[o_shape, o_shape],
        compiler_params=_cparams(("parallel", "arbitrary")), name="rwkv_proj_key",
    )(x2d, w.astype(BF16), kk_scale.reshape(1, n))


def _lora_kernel(h_ref, w2_ref, b_ref, o_ref, *, mode, hid):
    h = h_ref[...]
    for e in range(o_ref.shape[0]):
        z = _dot(h[:, e * hid:(e + 1) * hid], w2_ref[e])
        if mode == "decay":
            z = -jax.nn.sigmoid(b_ref[e] + z) * jnp.exp(F32(-0.5))
        elif mode == "iclr":
            z = jax.nn.sigmoid(b_ref[e] + z)
        o_ref[e] = z


def _lora(h2d, w2, bias, mode):
    t = h2d.shape[0]
    ne, rank, dim = w2.shape
    hid = h2d.shape[1] // ne
    w2p = jnp.pad(w2, ((0, 0), (0, hid - rank), (0, 0))).astype(BF16)
    tm = min(t, 512)
    return pl.pallas_call(
        functools.partial(_lora_kernel, mode=mode, hid=hid),
        grid=(t // tm,),
        in_specs=[
            pl.BlockSpec((tm, ne * hid), lambda i: (i, 0)),
            pl.BlockSpec((ne, hid, dim), lambda i: (0, 0, 0)),
            pl.BlockSpec((ne, 1, dim), lambda i: (0, 0, 0)),
        ],
        out_specs=pl.BlockSpec((ne, tm, dim), lambda i: (0, i, 0)),
        out_shape=jax.ShapeDtypeStruct((ne, t, dim), F32),
        compiler_params=_cparams(("parallel",)),
        name="rwkv_lora_" + mode,
    )(h2d, w2p, bias.reshape(ne, 1, dim))


def _pair_blockdiag(x):
    lane = lax.broadcasted_iota(jnp.int32, x.shape, 1)
    top = jnp.where(lane < HEAD_SIZE, x, 0.0)
    bot = jnp.where(lane >= HEAD_SIZE, x, 0.0)
    return jnp.concatenate([top, bot], axis=0).astype(BF16)


def _pair_dot(a, b):
    return _dot(a.astype(BF16), _pair_blockdiag(b))


def _wkv_kernel(r_ref, k_ref, v_ref, kk_ref, lw_ref, a_ref, ka_ref, *rest):
    n_cast = (len(rest) - 2) // 2
    y_ref, s_ref = rest[n_cast], rest[-1]
    for w_ref, wo_ref in zip(rest[:n_cast], rest[n_cast + 1:-1]):
        wo_ref[...] = w_ref[...].astype(BF16)
    d = pl.program_id(0)
    sgn = 1 - 2 * d
    fwd_f = (1 - d).astype(F32)
    L = CHUNK

    @pl.when(pl.program_id(2) == 0)
    def _():
        s_ref[...] = jnp.zeros_like(s_ref)

    t_sq = lax.broadcasted_iota(jnp.int32, (L, L), 0)
    s_sq = lax.broadcasted_iota(jnp.int32, (L, L), 1)
    tri = (((t_sq - s_sq) * sgn) >= 0).astype(BF16)
    t_p = lax.broadcasted_iota(jnp.int32, (L, LANES), 0)
    s_p = lax.broadcasted_iota(jnp.int32, (L, LANES), 1) % HEAD_SIZE
    rel = (t_p - s_p) * sgn
    m_strict = rel > 0
    m_read = (rel + (1 - d)) > 0
    eye = (t_p == s_p).astype(F32)
    row_h = lax.broadcasted_iota(jnp.int32, (LANES, LANES), 0) // HEAD_SIZE
    col_h = lax.broadcasted_iota(jnp.int32, (LANES, LANES), 1) // HEAD_SIZE
    same_head = row_h == col_h
    n_pairs = s_ref.shape[0]
    n_sub = r_ref.shape[0] // L
    sub_rows = [pl.ds(pl.multiple_of((i + d * (n_sub - 1 - 2 * i)) * L, L), L) for i in range(n_sub)]
    cols = [slice(p * LANES, (p + 1) * LANES) for p in range(n_pairs)]
    rows = [rw for rw in sub_rows for _ in cols]
    ccol = [cs for _ in sub_rows for cs in cols]

    def each(fn, *lists):
        return [fn(*args) for args in zip(*lists)]

    def load(ref):
        return [ref[rw, cs] for rw, cs in zip(rows, ccol)]

    def cumulative(lw):
        m = _dot(tri, jnp.concatenate(_split3(lw), axis=1))
        return m[:, :LANES] + m[:, LANES:2 * LANES] + m[:, 2 * LANES:]

    def halves(m):
        return m[:, :LANES], m[:, LANES:]

    def lanes2(x, y):
        return jnp.concatenate([_pair_blockdiag(x), _pair_blockdiag(y)], axis=1)

    lw = load(lw_ref)
    c_incl = each(cumulative, lw)
    c_excl = each(lambda c, w: c - w, c_incl, lw)
    tot = each(lambda w: jnp.sum(w, axis=0, keepdims=True), lw)
    kk = load(kk_ref)
    a = load(a_ref)
    beta = each(lambda x, y: x * y, kk, a)
    kt = each(lambda x, a_p, cs: x * (1.0 + (a_p - 1.0) * ka_ref[:, cs]), load(k_ref), a, ccol)
    ah = each(lambda x, c: -x * jnp.exp(c), kk, c_excl)
    rh = each(lambda x, c, w: x * jnp.exp(c + fwd_f * w), load(r_ref), c_excl, lw)
    e_neg = each(lambda c: jnp.exp(-c), c_incl)
    lhs = each(lambda x, y: jnp.concatenate([x, y], axis=0).astype(BF16), ah, rh)
    a_bk = each(lambda l, b, k_, e: _dot_nt(l, jnp.concatenate([_pair_blockdiag(b * e), _pair_blockdiag(k_ * e)],
                                                               axis=0)), lhs, beta, kt, e_neg)
    n_ab = each(lambda m: jnp.where(m_strict, m[:L, :LANES], 0.0), a_bk)
    n_ak = each(lambda m: jnp.where(m_strict, m[:L, LANES:], 0.0), a_bk)
    n_rb = each(lambda m: jnp.where(m_read, m[L:, :LANES], 0.0), a_bk)
    n_rk = each(lambda m: jnp.where(m_read, m[L:, LANES:], 0.0), a_bk)
    v = load(v_ref)
    xy = each(lambda x, y, v_p: _dot(jnp.concatenate([x, y], axis=0).astype(BF16), _pair_blockdiag(v_p)),
              n_ak, n_rk, v)
    tm = each(lambda n: eye + n, n_ab)
    pw = each(_pair_dot, n_ab, n_ab)
    for _ in range(1, L.bit_length() - 2):
        st = each(lambda q, t: _dot(jnp.concatenate([q, t], axis=0).astype(BF16), _pair_blockdiag(q)), pw, tm)
        tm = each(lambda t, m: t + m[L:], tm, st)
        pw = each(lambda m: m[:L], st)
    tm = each(lambda t, q: t + _pair_dot(t, q), tm, pw)
    at_ui = each(lambda t, x, m: halves(_dot(t.astype(BF16), lanes2(x, m[:L]))), tm, ah, xy)
    rt_yi = each(lambda n, au: halves(_dot(n.astype(BF16), lanes2(au[0], au[1]))), n_rb, at_ui)
    rt = each(lambda x, ry: x + ry[0], rh, rt_yi)
    yi = each(lambda ry, m: ry[1] + m[L:], rt_yi, xy)
    e_bar = each(lambda t, c: jnp.exp(t - c), tot, c_incl)
    rhs = each(lambda b, k_, e: jnp.concatenate([b * e, k_ * e], axis=0).astype(BF16), beta, kt, e_bar)
    for i, rw in enumerate(sub_rows):
        ch = slice(i * n_pairs, (i + 1) * n_pairs)
        s = [s_ref[p] for p in range(n_pairs)]
        uy = each(lambda au, x, s_p: _dot_nt(jnp.concatenate([au[0], x], axis=0).astype(BF16), s_p.astype(BF16)),
                  at_ui[ch], rt[ch], s)
        for cs, uy_p, yi_p in zip(cols, uy, yi[ch]):
            y_ref[rw, cs] = uy_p[L:] + yi_p
        uv_t = each(lambda uy_p, au, v_p: jnp.concatenate([uy_p[:L] + au[1], v_p], axis=0).T.astype(BF16),
                    uy, at_ui[ch], v[ch])
        upd = each(_dot, uv_t, rhs[ch])
        for p in range(n_pairs):
            s_ref[p] = s[p] * jnp.exp(tot[ch][p]) + jnp.where(same_head, upd[p], 0.0)


BF16_SUBLANES = 16
WKV_CHUNKS_PER_STEP = 2


def _wkv(r, k, v, kk, lw, a, k_a, cast_weights=()):
    bsz, seq, dim = r.shape
    sub = WKV_CHUNKS_PER_STEP if (seq // CHUNK) % WKV_CHUNKS_PER_STEP == 0 else 1
    rows = sub * CHUNK
    nc = seq // rows
    steps = 2 * bsz * nc

    def cidx(d, c):
        return c + d * (nc - 1 - 2 * c)

    def step(d, b, c):
        return (d * bsz + b) * nc + c

    flat = [w.reshape(-1, w.shape[-1]) for w in cast_weights]
    slab = [w.shape[0] // steps for w in flat]
    in_kernel = all(w.shape[0] % steps == 0 and s % BF16_SUBLANES == 0 for w, s in zip(flat, slab))
    if not in_kernel:
        flat, slab = [], []
    w_specs = [pl.BlockSpec((s, w.shape[1]), lambda d, b, c: (step(d, b, c), 0)) for w, s in zip(flat, slab)]
    tok = pl.BlockSpec((None, rows, dim), lambda d, b, c: (b, cidx(d, c), 0))
    dirtok = pl.BlockSpec((None, None, rows, dim), lambda d, b, c: (d, b, cidx(d, c), 0))
    outs = pl.pallas_call(
        _wkv_kernel,
        grid=(2, bsz, nc),
        in_specs=[tok, tok, tok, tok, dirtok, dirtok, pl.BlockSpec((1, dim), lambda d, b, c: (0, 0))] + w_specs,
        out_specs=[dirtok] + w_specs,
        out_shape=[jax.ShapeDtypeStruct((2, bsz, seq, dim), F32)]
        + [jax.ShapeDtypeStruct(w.shape, BF16) for w in flat],
        scratch_shapes=[pltpu.VMEM((dim // LANES, LANES, LANES), F32)],
        compiler_params=_cparams(("arbitrary", "arbitrary", "arbitrary")),
        name="wkv_scan",
    )(r, k, v, kk, lw, a, k_a.reshape(1, dim), *flat)
    if in_kernel:
        return outs[0], [o.reshape(w.shape) for o, w in zip(outs[1:], cast_weights)]
    return outs[0], [w.astype(BF16) for w in cast_weights]


def _rwkv_out_kernel(yf_ref, yb_ref, r_ref, k_ref, v_ref, a_ref, g_ref, h_ref,
                     ka_ref, rk_ref, lw_ref, lb_ref, wo_ref, o_ref, z_ref):
    ones_bd = _head_ones()
    inv_n = 1.0 / HEAD_SIZE
    for c in range(h_ref.shape[1] // HEAD_SUM_LANES):
        cs = slice(c * HEAD_SUM_LANES, (c + 1) * HEAD_SUM_LANES)
        y = yf_ref[:, cs] + yb_ref[:, cs]
        mu = _head_sum(y, ones_bd) * inv_n
        yc = y - mu
        var = _head_sum(yc * yc, ones_bd) * inv_n
        yn = yc * lax.rsqrt(var + LNX_EPS) * lw_ref[:, cs] + lb_ref[:, cs]
        kd = k_ref[:, cs] * (1.0 + (a_ref[:, cs] - 1.0) * ka_ref[:, cs])
        bonus = _head_sum(r_ref[:, cs] * kd * rk_ref[:, cs], ones_bd) * v_ref[:, cs]
        z_ref[:, cs] = ((yn + bonus) * g_ref[:, cs]).astype(BF16)
    o_ref[...] = h_ref[...] + _dot(z_ref[...], wo_ref[...])


def _rwkv_out(y, r, k, v, a, g, h2d, k_a, r_k, lnx_w, lnx_b, w_o):
    t, dim = h2d.shape
    tm = min(t, 256)
    tok = pl.BlockSpec((tm, dim), lambda i: (i, 0))
    vec = pl.BlockSpec((1, dim), lambda i: (0, 0))
    return pl.pallas_call(
        _rwkv_out_kernel,
        grid=(t // tm,),
        in_specs=[pl.BlockSpec((None, tm, dim), lambda i: (0, i, 0)),
                  pl.BlockSpec((None, tm, dim), lambda i: (1, i, 0)),
                  tok, tok, tok,
                  pl.BlockSpec((None, tm, dim), lambda i: (0, i, 0)),
                  tok, tok, vec, vec, vec, vec,
                  pl.BlockSpec((dim, dim), lambda i: (0, 0))],
        out_specs=tok,
        out_shape=jax.ShapeDtypeStruct((t, dim), F32),
        scratch_shapes=[pltpu.VMEM((tm, dim), BF16)],
        compiler_params=_cparams(("parallel",)),
        name="rwkv_out",
    )(y, y, r, k, v, a, g, h2d, k_a.reshape(1, dim), r_k.reshape(1, dim),
      lnx_w.reshape(1, dim), lnx_b.reshape(1, dim), w_o.astype(BF16))


ROUTE_E1, ROUTE_E2, ROUTE_G1, ROUTE_G2, ROUTE_R1, ROUTE_R2 = range(6)


def _router_kernel(x_ref, g_ref, w_ref, hn_ref, route_ref, cnt_ref, *, n_exp, tm):
    i = pl.program_id(0)

    @pl.when(i == 0)
    def _():
        cnt_ref[...] = jnp.zeros_like(cnt_ref)

    hn = _rms(x_ref[...], g_ref[...])
    for c in range(hn_ref.shape[1]):
        hn_ref[:, c, :] = hn[:, c * LANES:(c + 1) * LANES]
    x_hi, x_mid, x_lo = _split3(hn)
    w_hi, w_mid, w_lo = w_ref[0], w_ref[1], w_ref[2]
    logits = (_dot(x_hi, w_hi) + _dot(x_hi, w_mid) + _dot(x_mid, w_hi)
              + _dot(x_hi, w_lo) + _dot(x_mid, w_mid) + _dot(x_lo, w_hi))
    lane = lax.broadcasted_iota(jnp.int32, (tm, LANES), 1)
    neg = F32(-jnp.inf)
    lg = jnp.where(lane < n_exp, logits, neg)
    m1 = jnp.max(lg, axis=-1, keepdims=True)
    i1 = jnp.min(jnp.where(lg == m1, lane, LANES), axis=-1, keepdims=True)
    lg2 = jnp.where(lane == i1, neg, lg)
    m2 = jnp.max(lg2, axis=-1, keepdims=True)
    i2 = jnp.min(jnp.where(lg2 == m2, lane, LANES), axis=-1, keepdims=True)
    e2 = jnp.exp(m2 - m1)
    den = 1.0 + e2
    g1 = 1.0 / den
    g2 = e2 / den
    oh1 = lane == i1
    oh2 = lane == i2
    oh = oh1.astype(F32) + oh2.astype(F32)
    rr = lax.broadcasted_iota(jnp.int32, (tm, tm), 0)
    cc = lax.broadcasted_iota(jnp.int32, (tm, tm), 1)
    before = (cc < rr).astype(BF16)
    prefix = _dot(before, oh.astype(BF16)) + cnt_ref[...]
    r1 = jnp.sum(jnp.where(oh1, prefix, 0.0), axis=-1, keepdims=True)
    r2 = jnp.sum(jnp.where(oh2, prefix, 0.0), axis=-1, keepdims=True)
    cnt_ref[...] += jnp.sum(oh, axis=0, keepdims=True)
    route = jnp.zeros((tm, LANES), F32)
    for col, val in ((ROUTE_E1, i1.astype(F32)), (ROUTE_E2, i2.astype(F32)), (ROUTE_G1, g1),
                     (ROUTE_G2, g2), (ROUTE_R1, r1), (ROUTE_R2, r2)):
        route = jnp.where(lane == col, val, route)
    route_ref[...] = route


def _router(h2d, g, router_w):
    t, dim = h2d.shape
    n_exp = router_w.shape[1]
    tm = min(t, 512)
    wp = jnp.pad(router_w, ((0, 0), (0, LANES - n_exp)))
    w3 = jnp.stack(_split3(wp))
    return pl.pallas_call(
        functools.partial(_router_kernel, n_exp=n_exp, tm=tm),
        grid=(t // tm,),
        in_specs=[pl.BlockSpec((tm, dim), lambda i: (i, 0)),
                  pl.BlockSpec((1, dim), lambda i: (0, 0)),
                  pl.BlockSpec((3, dim, LANES), lambda i: (0, 0, 0))],
        out_specs=[pl.BlockSpec((tm, dim // LANES, LANES), lambda i: (i, 0, 0)),
                   pl.BlockSpec((tm, LANES), lambda i: (i, 0)),
                   pl.BlockSpec((1, LANES), lambda i: (0, 0))],
        out_shape=[jax.ShapeDtypeStruct((t, dim // LANES, LANES), F32),
                   jax.ShapeDtypeStruct((t, LANES), F32),
                   jax.ShapeDtypeStruct((1, LANES), F32)],
        compiler_params=_cparams(("arbitrary",)),
        name="moe_router",
    )(h2d, g.reshape(1, dim), w3)


def _row_gather(src_hbm, dst_vmem, sem, index_of, rows):
    def copy(t):
        return pltpu.make_async_copy(src_hbm.at[pl.ds(index_of(t), 1)], dst_vmem.at[pl.ds(t, 1)], sem)

    def start():
        lax.fori_loop(0, rows, lambda t, c: (copy(t).start(), c)[1], 0, unroll=8)

    def wait():
        for t in range(rows):
            pltpu.make_async_copy(src_hbm.at[pl.ds(0, 1)], dst_vmem.at[pl.ds(t, 1)], sem).wait()

    return start, wait


def _dispatch_kernel(src_ref, hn_ref, xs_ref, buf_ref, sem, *, rows):
    i = pl.program_id(0)
    n = pl.num_programs(0)

    def gather(step):
        slot = step % 2
        return _row_gather(hn_ref, buf_ref.at[slot], sem.at[slot], lambda t: src_ref[step * rows + t], rows)

    @pl.when(i == 0)
    def _():
        gather(i)[0]()

    @pl.when(i + 1 < n)
    def _():
        gather(i + 1)[0]()

    gather(i)[1]()
    slot = i % 2
    for c in range(buf_ref.shape[2]):
        xs_ref[:, c * LANES:(c + 1) * LANES] = buf_ref[slot, :, c, :].astype(BF16)


def _dispatch(hn, src_rows):
    tile_shape = hn.shape[1:]
    dim = tile_shape[0] * tile_shape[1]
    n_rows = src_rows.shape[0]
    rows = min(n_rows, 512)
    return pl.pallas_call(
        functools.partial(_dispatch_kernel, rows=rows),
        grid_spec=pltpu.PrefetchScalarGridSpec(
            num_scalar_prefetch=1,
            grid=(n_rows // rows,),
            in_specs=[pl.BlockSpec(memory_space=pl.ANY)],
            out_specs=pl.BlockSpec((rows, dim), lambda i, s: (i, 0)),
            scratch_shapes=[pltpu.VMEM((2, rows) + tile_shape, F32), pltpu.SemaphoreType.DMA((2,))],
        ),
        out_shape=jax.ShapeDtypeStruct((n_rows, dim), BF16),
        compiler_params=_cparams(("arbitrary",)),
        name="moe_dispatch",
    )(src_rows, hn)


def _expert_kernel(te_ref, nu_ref, x_ref, wg_ref, wu_ref, wd_ref, o_ref, acc_ref):
    del te_ref
    i = pl.program_id(0)
    j = pl.program_id(1)
    used = i < nu_ref[0]

    @pl.when(j == 0)
    def _():
        acc_ref[...] = jnp.zeros_like(acc_ref)

    @pl.when(used)
    def _():
        x = x_ref[...]
        gate = _dot(x, wg_ref[...])
        up = _dot(x, wu_ref[...])
        act = (gate * jax.nn.sigmoid(gate) * up).astype(BF16)
        acc_ref[...] += _dot(act, wd_ref[...])

    @pl.when(j == pl.num_programs(1) - 1)
    def _():
        for c in range(o_ref.shape[1]):
            o_ref[:, c, :] = acc_ref[:, c * LANES:(c + 1) * LANES]


def _expert_tiles(ff):
    for tf in (1024, 512, 256, 128):
        if ff % tf == 0:
            return tf
    return ff


def _experts(xs, tile_expert, n_used, w_gate, w_up, w_down, tm):
    n_rows, dim = xs.shape
    ff = w_gate.shape[2]
    tf = _expert_tiles(ff)
    nj = ff // tf

    def jj(i, j, nu):
        return jnp.where(i < nu[0], j, nj - 1)

    return pl.pallas_call(
        _expert_kernel,
        grid_spec=pltpu.PrefetchScalarGridSpec(
            num_scalar_prefetch=2,
            grid=(n_rows // tm, nj),
            in_specs=[pl.BlockSpec((tm, dim), lambda i, j, te, nu: (i, 0)),
                      pl.BlockSpec((None, dim, tf), lambda i, j, te, nu: (te[i], 0, jj(i, j, nu))),
                      pl.BlockSpec((None, dim, tf), lambda i, j, te, nu: (te[i], 0, jj(i, j, nu))),
                      pl.BlockSpec((None, tf, dim), lambda i, j, te, nu: (te[i], jj(i, j, nu), 0))],
            out_specs=pl.BlockSpec((tm, dim // LANES, LANES), lambda i, j, te, nu: (i, 0, 0)),
            scratch_shapes=[pltpu.VMEM((tm, dim), F32)],
        ),
        out_shape=jax.ShapeDtypeStruct((n_rows, dim // LANES, LANES), F32),
        compiler_params=_cparams(("arbitrary", "arbitrary")),
        name="moe_experts",
    )(tile_expert, n_used, xs, w_gate.astype(BF16), w_up.astype(BF16), w_down.astype(BF16))


def _combine_kernel(slots_ref, h_ref, route_ref, g_ref, y_ref, o_ref, buf_ref, sem, *, rows):
    i = pl.program_id(0)
    n = pl.num_programs(0)

    def gather(step, k):
        slot = step % 2
        return _row_gather(y_ref, buf_ref.at[slot, k], sem.at[slot],
                           lambda t: slots_ref[TOP_K * (step * rows + t) + k], rows)

    @pl.when(i == 0)
    def _():
        for k in range(TOP_K):
            gather(i, k)[0]()

    @pl.when(i + 1 < n)
    def _():
        for k in range(TOP_K):
            gather(i + 1, k)[0]()

    for k in range(TOP_K):
        gather(i, k)[1]()
    slot = i % 2
    route = route_ref[...]
    g1 = route[:, ROUTE_G1:ROUTE_G1 + 1]
    g2 = route[:, ROUTE_G2:ROUTE_G2 + 1]
    dim = h_ref.shape[1]
    ss = jnp.zeros((rows, 1), F32)
    for c in range(dim // LANES):
        cs = slice(c * LANES, (c + 1) * LANES)
        hc = h_ref[:, cs] + g1 * buf_ref[slot, 0, :, c, :] + g2 * buf_ref[slot, 1, :, c, :]
        o_ref[:, cs] = hc
        ss = ss + jnp.sum(hc * hc, axis=-1, keepdims=True)
    o_ref[...] = o_ref[...] * lax.rsqrt(ss * (1.0 / dim) + RMS_EPS) * g_ref[...]


def _combine(h2d, route, slots, y, g):
    t, dim = h2d.shape
    tile_shape = y.shape[1:]
    rows = min(t, 256)
    return pl.pallas_call(
        functools.partial(_combine_kernel, rows=rows),
        grid_spec=pltpu.PrefetchScalarGridSpec(
            num_scalar_prefetch=1,
            grid=(t // rows,),
            in_specs=[pl.BlockSpec((rows, dim), lambda i, s: (i, 0)),
                      pl.BlockSpec((rows, LANES), lambda i, s: (i, 0)),
                      pl.BlockSpec((1, dim), lambda i, s: (0, 0)),
                      pl.BlockSpec(memory_space=pl.ANY)],
            out_specs=pl.BlockSpec((rows, dim), lambda i, s: (i, 0)),
            scratch_shapes=[pltpu.VMEM((2, TOP_K, rows) + tile_shape, F32), pltpu.SemaphoreType.DMA((2,))],
        ),
        out_shape=jax.ShapeDtypeStruct((t, dim), F32),
        compiler_params=_cparams(("arbitrary",)),
        name="moe_combine",
    )(slots, h2d, route, g.reshape(1, dim), y)


def _moe_layer(h2d, norm_g, router_w, w_gate, w_up, w_down, final_g):
    t, dim = h2d.shape
    n_exp = router_w.shape[1]
    tm = min(t, 512)
    hn, route, counts = _router(h2d, norm_g, router_w)
    counts = counts[0, :n_exp].astype(jnp.int32)
    tiles = (counts + tm - 1) // tm
    tile_end = jnp.cumsum(tiles)
    row_start = (tile_end - tiles) * tm
    e12 = route[:, ROUTE_E1:ROUTE_E2 + 1].astype(jnp.int32)
    r12 = route[:, ROUTE_R1:ROUTE_R2 + 1].astype(jnp.int32)
    slots = (row_start[e12] + r12).reshape(-1)
    n_tiles = TOP_K * t // tm + n_exp
    n_used = tile_end[-1:]
    tile_ids = jnp.minimum(jnp.arange(n_tiles, dtype=jnp.int32), n_used[0] - 1)
    tile_expert = jnp.sum(tile_ids[:, None] >= tile_end[None, :], axis=1).astype(jnp.int32)
    token = jnp.repeat(jnp.arange(t, dtype=jnp.int32), TOP_K)
    src_rows = jnp.zeros((n_tiles * tm,), jnp.int32).at[slots].set(token)
    xs = _dispatch(hn, src_rows)
    y = _experts(xs, tile_expert, n_used, w_gate, w_up, w_down, tm)
    return _combine(h2d, route, slots, y, final_g)


def kernel(x, l0_norm_mix, l0_pool_w, l0_pool_scale, l0_norm_ffn, l0_ffn_w_gate, l0_ffn_w_up, l0_ffn_w_down, l1_norm_mix, l1_mix, l1_w_r, l1_w_k, l1_w_v, l1_w_o, l1_decay_w0, l1_decay_w1, l1_decay_w2, l1_iclr_a0, l1_iclr_a1, l1_iclr_a2, l1_gate_g1, l1_gate_g2, l1_k_k, l1_k_a, l1_r_k, l1_lnx_w, l1_lnx_b, l1_norm_ffn, l1_router, l1_moe_w_gate, l1_moe_w_up, l1_moe_w_down, final_norm):
    bsz, seq, dim = x.shape
    t = bsz * seq
    h = _pool_layer(x, l0_norm_mix, l0_pool_w, l0_pool_scale)
    h = _ffn_layer(h.reshape(t, dim), l0_norm_ffn, l0_ffn_w_gate, l0_ffn_w_up, l0_ffn_w_down)
    xr, xk, xv, hw, ha, hg = [m.reshape(t, m.shape[-1]) for m in _rwkv_prep(
        h.reshape(bsz, seq, dim), l1_norm_mix, l1_mix, l1_decay_w1, l1_iclr_a1, l1_gate_g1[None])]
    r = _proj(xr, l1_w_r)
    k, kk = _proj(xk, l1_w_k, kk_scale=l1_k_k)
    v = _proj(xv, l1_w_v)
    lw = _lora(hw, l1_decay_w2, l1_decay_w0, "decay")
    a = _lora(ha, l1_iclr_a2, l1_iclr_a0, "iclr")
    g = _lora(hg, l1_gate_g2[None], jnp.zeros((1, dim), F32), "gate")[0]
    b3 = lambda z: z.reshape(bsz, seq, dim)
    b4 = lambda z: z.reshape(2, bsz, seq, dim)
    y, (moe_wg, moe_wu, moe_wd) = _wkv(b3(r), b3(k), b3(v), b3(kk), b4(lw), b4(a), l1_k_a,
                                       cast_weights=(l1_moe_w_gate, l1_moe_w_up, l1_moe_w_down))
    h = _rwkv_out(y.reshape(2, t, dim), r, k, v, a, g, h, l1_k_a, l1_r_k, l1_lnx_w, l1_lnx_b, l1_w_o)
    out = _moe_layer(h, l1_norm_ffn, l1_router, moe_wg, moe_wu, moe_wd, final_norm)
    return out.reshape(bsz, seq, dim)
```
